```python
import math
import jax, jax.numpy as jnp
from jax import lax
import numpy as np

D_MODEL = 1024
BATCH = 8
SEQ = 2048
DEPTH = 2
DEC_BATCH = 32
DEC_SEQ = 8
PAST_LEN = 16384
PAGE_SIZE = 128

MIX_DIM = D_MODEL // 2
CONV_DIM = MIX_DIM
CONV_W = 31
N_HEADS = 8
HEAD_DIM = MIX_DIM // N_HEADS
ATT_DIM = N_HEADS * HEAD_DIM
POOL_DIM = MIX_DIM
POOL_WINDOWS = (2, 4, 8, 16)
N_POOL_GROUPS = len(POOL_WINDOWS)
POOL_GROUP = POOL_DIM // N_POOL_GROUPS
POOL_OUT_GROUP = D_MODEL // N_POOL_GROUPS
POOL_MAX = max(POOL_WINDOWS)
N_BRANCH = 3
IN_COLS = 2 * CONV_DIM + 3 * ATT_DIM + POOL_DIM + N_BRANCH * D_MODEL
D_FF = ((8 * D_MODEL // 3 + 127) // 128) * 128
N_EXPERTS = 8
TOP_K = 2
EXPERT_FF = D_FF
N_DENSE = (DEPTH + 1) // 2
N_MOE = DEPTH // 2
Q_BLOCK = 128
SB_SCALE = 1.0 / math.sqrt(HEAD_DIM)
SB_BIAS_INIT = -6.0
EPS = 1e-6

kernel_name = "hybrid_conv_stickbreak_pool_decoder_step"


def rms_norm(x, g):
    xf = x.astype(jnp.float32)
    y = xf * lax.rsqrt(jnp.mean(xf * xf, axis=-1, keepdims=True) + EPS)
    return (y * g.astype(jnp.float32)).astype(x.dtype)


def swiglu(x, w1, w3, w2):
    return (jax.nn.silu(x @ w1) * (x @ w3)) @ w2


def moe_swiglu(x, w_router, w1, w3, w2):
    logits = (x @ w_router).astype(jnp.float32)
    top_val, top_idx = lax.top_k(logits, TOP_K)
    probs = jax.nn.softmax(top_val, axis=-1)
    gate = jnp.sum(jax.nn.one_hot(top_idx, N_EXPERTS, dtype=jnp.float32) * probs[..., None], axis=-2)
    out = jnp.zeros_like(x)
    for e in range(N_EXPERTS):
        out = out + gate[..., e:e + 1].astype(x.dtype) * swiglu(x, w1[e], w3[e], w2[e])
    return out


def sb_weights(z, mask):
    l1m = jnp.where(mask, jax.nn.log_sigmoid(-z), 0.0)
    tail = lax.cumsum(l1m, axis=z.ndim - 1, reverse=True) - l1m
    return jnp.where(mask, jnp.exp(jax.nn.log_sigmoid(z) + tail), 0.0)


def sb_attn_prompt(q, k, v, bias):
    B, T, H, Dh = q.shape
    qb = min(Q_BLOCK, T)
    nb = T // qb
    q_blocks = jnp.moveaxis(q.reshape(B, nb, qb, H, Dh), 1, 0)
    k_pos = jnp.arange(T)
    b_h = bias.astype(jnp.float32)[None, :, None, None]

    def one_block(args):
        q_blk, start = args
        q_pos = start + jnp.arange(qb)
        z = jnp.einsum('bqhd,bkhd->bhqk', q_blk, k).astype(jnp.float32) * SB_SCALE + b_h
        w = sb_weights(z, k_pos[None, :] < q_pos[:, None])
        return jnp.einsum('bhqk,bkhd->bqhd', w.astype(v.dtype), v)

    out = lax.map(one_block, (q_blocks, jnp.arange(nb) * qb))
    return jnp.moveaxis(out, 0, 1).reshape(B, T, H * Dh)


def sb_attn_sample(q, k, v, bias, k_past, v_past):
    B, T, H, Dh = q.shape
    P = k_past.shape[1]
    b_h = bias.astype(jnp.float32)[None, :, None, None]
    z = jnp.concatenate([jnp.einsum('bqhd,bkhd->bhqk', q, k_past),
                         jnp.einsum('bqhd,bkhd->bhqk', q, k)], axis=-1).astype(jnp.float32) * SB_SCALE + b_h
    q_pos = P + jnp.arange(T)
    k_pos = jnp.arange(P + T)
    w = sb_weights(z, k_pos[None, :] < q_pos[:, None]).astype(v.dtype)
    out = (jnp.einsum('bhqk,bkhd->bqhd', w[..., :P], v_past)
           + jnp.einsum('bhqk,bkhd->bqhd', w[..., P:], v))
    return out.reshape(B, T, H * Dh)


def conv_branch(u_ext, conv_w, conv_b, ln_g, ln_b, w_out):
    y = lax.conv_general_dilated(u_ext, conv_w[:, None, :], window_strides=(1,), padding='VALID',
                                 dimension_numbers=('NWC', 'WIO', 'NWC'),
                                 feature_group_count=CONV_DIM) + conv_b
    yf = y.astype(jnp.float32)
    mu = jnp.mean(yf, axis=-1, keepdims=True)
    var = jnp.mean(jnp.square(yf - mu), axis=-1, keepdims=True)
    yn = ((yf - mu) * lax.rsqrt(var + EPS) * ln_g.astype(jnp.float32) + ln_b.astype(jnp.float32)).astype(u_ext.dtype)
    return jax.nn.silu(yn) @ w_out


def pool_branch(p_ext, pos0, pool_w, pool_scale):
    B, L, _ = p_ext.shape
    T = L - (POOL_MAX - 1)
    pf = p_ext.astype(jnp.float32)
    cs = jnp.concatenate([jnp.zeros((B, 1, POOL_DIM), jnp.float32), jnp.cumsum(pf, axis=1)], axis=1)
    pos = pos0 + jnp.arange(T)
    cur = pf[:, POOL_MAX - 1:]
    outs = []
    for g, w in enumerate(POOL_WINDOWS):
        sl = slice(g * POOL_GROUP, (g + 1) * POOL_GROUP)
        win_sum = cs[:, POOL_MAX:POOL_MAX + T, sl] - cs[:, POOL_MAX - w:POOL_MAX - w + T, sl]
        cnt = jnp.minimum(pos + 1, w).astype(jnp.float32)[None, :, None]
        outs.append(win_sum / cnt - cur[..., sl])
    pooled = jnp.stack(outs, axis=2).astype(p_ext.dtype)
    y = jnp.einsum('btgc,gce->btge', pooled, pool_w).reshape(B, T, D_MODEL)
    return y * pool_scale


def token_mixing(x, conv_prev, pool_prev, pos0, attend, norm_g, w_in, conv_w, conv_b,
                 ln_g, ln_b, w_conv_out, w_attn_out, sb_bias, pool_w, pool_scale, w_o):
    B, T, _ = x.shape
    hn = rms_norm(x, norm_g)
    proj = hn @ w_in
    o1 = 2 * CONV_DIM
    o2 = o1 + ATT_DIM
    o3 = o2 + ATT_DIM
    o4 = o3 + ATT_DIM
    o5 = o4 + POOL_DIM
    glu, q, k, v, p, gates = jnp.split(proj, [o1, o2, o3, o4, o5], axis=-1)
    u = glu[..., :CONV_DIM] * jax.nn.sigmoid(glu[..., CONV_DIM:])
    q = q.reshape(B, T, N_HEADS, HEAD_DIM)
    k = k.reshape(B, T, N_HEADS, HEAD_DIM)
    v = v.reshape(B, T, N_HEADS, HEAD_DIM)
    u_ext = jnp.concatenate([conv_prev.astype(u.dtype), u], axis=1)
    p_ext = jnp.concatenate([pool_prev.astype(p.dtype), p], axis=1)
    y_conv = conv_branch(u_ext, conv_w, conv_b, ln_g, ln_b, w_conv_out)
    y_attn = attend(q, k, v, sb_bias) @ w_attn_out
    y_pool = pool_branch(p_ext, pos0, pool_w, pool_scale)
    g = jax.nn.sigmoid(gates).reshape(B, T, N_BRANCH, D_MODEL)
    merged = g[:, :, 0] * y_conv + g[:, :, 1] * y_attn + g[:, :, 2] * y_pool
    return (x + merged @ w_o, k, v, u_ext[:, -(CONV_W - 1):], p_ext[:, -(POOL_MAX - 1):])


def setup_inputs(seed: int = 0) -> dict:
    key = jax.random.key(seed)
    ks = jax.random.split(key, 32)
    n_pages = PAST_LEN // PAGE_SIZE
    n_pool_pages = (DEC_BATCH * n_pages * 5) // 4
    nrm = lambda k, s, sc: jax.random.normal(k, s, jnp.float32) * sc
    perm = jax.random.permutation(ks[0], n_pool_pages)[:DEC_BATCH * n_pages]
    return {
        'x_prompt': nrm(ks[1], (BATCH, SEQ, D_MODEL), 1.0),
        'x_sample': nrm(ks[2], (DEC_BATCH, DEC_SEQ, D_MODEL), 1.0),
        'cache_k': nrm(ks[3], (DEPTH, n_pool_pages, PAGE_SIZE, N_HEADS, HEAD_DIM), 1.0),
        'cache_v': nrm(ks[4], (DEPTH, n_pool_pages, PAGE_SIZE, N_HEADS, HEAD_DIM), 1.0),
        'state_conv': nrm(ks[5], (DEPTH, DEC_BATCH, CONV_W - 1, CONV_DIM), 0.5),
        'state_pool': nrm(ks[6], (DEPTH, DEC_BATCH, POOL_MAX - 1, POOL_DIM), 1.0),
        'page_table': perm.reshape(DEC_BATCH, n_pages).astype(jnp.int32),
        'norm_mix': 1.0 + nrm(ks[7], (DEPTH, D_MODEL), 0.02),
        'w_in': nrm(ks[8], (DEPTH, D_MODEL, IN_COLS), D_MODEL ** -0.5),
        'conv_w': nrm(ks[9], (DEPTH, CONV_W, CONV_DIM), CONV_W ** -0.5),
        'conv_b': nrm(ks[10], (DEPTH, CONV_DIM), 0.02),
        'conv_ln_g': 1.0 + nrm(ks[11], (DEPTH, CONV_DIM), 0.02),
        'conv_ln_b': nrm(ks[12], (DEPTH, CONV_DIM), 0.02),
        'w_conv_out': nrm(ks[13], (DEPTH, CONV_DIM, D_MODEL), CONV_DIM ** -0.5),
        'w_attn_out': nrm(ks[14], (DEPTH, ATT_DIM, D_MODEL), ATT_DIM ** -0.5),
        'sb_bias': SB_BIAS_INIT + nrm(ks[27], (DEPTH, N_HEADS), 0.1),
        'pool_w': nrm(ks[15], (DEPTH, N_POOL_GROUPS, POOL_GROUP, POOL_OUT_GROUP), POOL_GROUP ** -0.5),
        'pool_scale': 1.0 + nrm(ks[16], (DEPTH, D_MODEL), 0.02),
        'w_o': nrm(ks[17], (DEPTH, D_MODEL, D_MODEL), D_MODEL ** -0.5),
        'norm_ffn': 1.0 + nrm(ks[18], (DEPTH, D_MODEL), 0.02),
        'ffn_w1': nrm(ks[19], (N_DENSE, D_MODEL, D_FF), D_MODEL ** -0.5),
        'ffn_w3': nrm(ks[20], (N_DENSE, D_MODEL, D_FF), D_MODEL ** -0.5),
        'ffn_w2': nrm(ks[21], (N_DENSE, D_FF, D_MODEL), D_FF ** -0.5),
        'moe_router': nrm(ks[22], (N_MOE, D_MODEL, N_EXPERTS), D_MODEL ** -0.5),
        'moe_w1': nrm(ks[23], (N_MOE, N_EXPERTS, D_MODEL, EXPERT_FF), D_MODEL ** -0.5),
        'moe_w3': nrm(ks[24], (N_MOE, N_EXPERTS, D_MODEL, EXPERT_FF), D_MODEL ** -0.5),
        'moe_w2': nrm(ks[25], (N_MOE, N_EXPERTS, EXPERT_FF, D_MODEL), EXPERT_FF ** -0.5),
        'norm_final': 1.0 + nrm(ks[26], (D_MODEL,), 0.02),
    }


def reference(x_prompt, x_sample, cache_k, cache_v, state_conv, state_pool, page_table,
              norm_mix, w_in, conv_w, conv_b, conv_ln_g, conv_ln_b, w_conv_out, w_attn_out,
              sb_bias, pool_w, pool_scale, w_o, norm_ffn, ffn_w1, ffn_w3, ffn_w2,
              moe_router, moe_w1, moe_w3, moe_w2, norm_final):
    Bp = x_prompt.shape[0]
    Bd, n_pages = page_table.shape
    past = n_pages * PAGE_SIZE
    xp, xs = x_prompt, x_sample
    kp_l, vp_l, cp_l, pp_l = [], [], [], []
    ks_l, vs_l, cs_l, ps_l = [], [], [], []
    for l in range(DEPTH):
        lw = (norm_mix[l], w_in[l], conv_w[l], conv_b[l], conv_ln_g[l], conv_ln_b[l],
              w_conv_out[l], w_attn_out[l], sb_bias[l], pool_w[l], pool_scale[l], w_o[l])
        zc = jnp.zeros((Bp, CONV_W - 1, CONV_DIM), xp.dtype)
        zp = jnp.zeros((Bp, POOL_MAX - 1, POOL_DIM), xp.dtype)
        xp, kp, vp, cp, pp = token_mixing(xp, zc, zp, 0, sb_attn_prompt, *lw)
        k_past = cache_k[l][page_table].reshape(Bd, past, N_HEADS, HEAD_DIM)
        v_past = cache_v[l][page_table].reshape(Bd, past, N_HEADS, HEAD_DIM)
        attend_s = lambda q, k, v, b, kp_=k_past, vp_=v_past: sb_attn_sample(q, k, v, b, kp_, vp_)
        xs, ks_, vs_, cs_, ps_ = token_mixing(xs, state_conv[l], state_pool[l], past, attend_s, *lw)
        kp_l.append(kp); vp_l.append(vp); cp_l.append(cp); pp_l.append(pp)
        ks_l.append(ks_); vs_l.append(vs_); cs_l.append(cs_); ps_l.append(ps_)
        hp = rms_norm(xp, norm_ffn[l])
        hs = rms_norm(xs, norm_ffn[l])
        i = l // 2
        if l % 2 == 0:
            xp = xp + swiglu(hp, ffn_w1[i], ffn_w3[i], ffn_w2[i])
            xs = xs + swiglu(hs, ffn_w1[i], ffn_w3[i], ffn_w2[i])
        else:
            xp = xp + moe_swiglu(hp, moe_router[i], moe_w1[i], moe_w3[i], moe_w2[i])
            xs = xs + moe_swiglu(hs, moe_router[i], moe_w1[i], moe_w3[i], moe_w2[i])
    y_prompt = rms_norm(xp, norm_final)
    y_sample = rms_norm(xs, norm_final)
    return (y_prompt, y_sample,
            jnp.stack(kp_l), jnp.stack(vp_l), jnp.stack(cp_l), jnp.stack(pp_l),
            jnp.stack(ks_l), jnp.stack(vs_l), jnp.stack(cs_l), jnp.stack(ps_l))
```

```python
import functools

import jax
import jax.numpy as jnp
from jax import lax
from jax.experimental import pallas as pl
from jax.experimental.pallas import tpu as pltpu

F32 = jnp.float32
BF16 = jnp.bfloat16
EPS = 1e-6

CONV_W = 31
POOL_WINDOWS = (2, 4, 8, 16)
POOL_MAX = 16
N_HEADS = 8
HEAD_DIM = 64
PAGE_SIZE = 128
TOP_K = 2
CONV_HIST = 32
POOL_HIST = 16
LANES = 128
VMEM_LIMIT = 56 * 1024 * 1024


def _params(sem):
    return pltpu.CompilerParams(dimension_semantics=sem, vmem_limit_bytes=VMEM_LIMIT)


def _resident(shape):
    zeros = (0,) * len(shape)
    return pl.BlockSpec(shape, lambda *_: zeros, pipeline_mode=pl.Buffered(1))


def _rms(x, g):
    return x * lax.rsqrt(jnp.mean(x * x, axis=-1, keepdims=True) + EPS) * g


def _softplus(z):
    return jnp.maximum(z, 0.0) + jnp.log1p(jnp.exp(-jnp.abs(z)))


def _split_bf16(x):
    hi = x.astype(BF16)
    return hi, (x - hi.astype(F32)).astype(BF16)


def _suffix_sum(tri, x):
    hi, lo = _split_bf16(x)
    return (jnp.dot(tri, hi, preferred_element_type=F32)
            + jnp.dot(tri, lo, preferred_element_type=F32))


def _in_proj_body(x_ref, g_ref, w_ref, u_ref, q_ref, k_ref, v_ref, p_ref, gate_ref, *, q_scale):
    hn = _rms(x_ref[...], g_ref[...]).astype(BF16)
    cd = u_ref.shape[1]
    ad = q_ref.shape[1]
    pd = p_ref.shape[1]

    def proj(c0, n):
        return jnp.dot(hn, w_ref[:, c0:c0 + n], preferred_element_type=F32)

    c = 0
    a = proj(c, cd)
    b = proj(c + cd, cd)
    u_ref[...] = a * jax.nn.sigmoid(b)
    c += 2 * cd
    q_ref[...] = (proj(c, ad) * q_scale).astype(BF16)
    c += ad
    k_ref[...] = proj(c, ad)
    c += ad
    v_ref[...] = proj(c, ad)
    c += ad
    p_ref[...] = proj(c, pd)
    c += pd
    ng = gate_ref.shape[1]
    step = 512
    for j in range(0, ng, step):
        gate_ref[:, j:j + step] = jax.nn.sigmoid(proj(c + j, step)).astype(BF16)


def _in_proj(x, g, w, tm):
    n, d = x.shape
    cd = ad = pd = 512
    ng = w.shape[1] - 2 * cd - 3 * ad - pd
    row = lambda i: (i, 0)
    outs = [
        jax.ShapeDtypeStruct((n, cd), F32),
        jax.ShapeDtypeStruct((n, ad), BF16),
        jax.ShapeDtypeStruct((n, ad), F32),
        jax.ShapeDtypeStruct((n, ad), F32),
        jax.ShapeDtypeStruct((n, pd), F32),
        jax.ShapeDtypeStruct((n, ng), BF16),
    ]
    return pl.pallas_call(
        functools.partial(_in_proj_body, q_scale=HEAD_DIM ** -0.5),
        grid=(n // tm,),
        in_specs=[pl.BlockSpec((tm, d), row), _resident((1, d)), _resident(w.shape)],
        out_specs=[pl.BlockSpec((tm, o.shape[1]), row) for o in outs],
        out_shape=outs,
        compiler_params=_params(("parallel",)),
        name="in_proj",
    )(x, g.reshape(1, d), w)


def _attn_prompt_body(bias_ref, q_ref, k_ref, v_ref, o_ref, vt_scr, acc_scr, *, tq, hp):
    t_len = q_ref.shape[0]
    nb = t_len // tq
    hd = HEAD_DIM
    g = pl.program_id(1)
    for j in range(nb):
        vt_scr[j] = v_ref[j * tq:(j + 1) * tq, :].T.astype(BF16)
    key = lax.broadcasted_iota(jnp.int32, (tq, tq), 0)
    qry = lax.broadcasted_iota(jnp.int32, (tq, tq), 1)
    valid = key < qry
    tri = (qry >= key).astype(BF16)

    def block(qi, kj, c, hh, bias, diag):
        qb = q_ref[pl.ds(pl.multiple_of(qi * tq, tq), tq), hh * hd:(hh + 1) * hd]
        kb = k_ref[pl.ds(pl.multiple_of(kj * tq, tq), tq), hh * hd:(hh + 1) * hd].astype(BF16)
        z = lax.dot_general(kb, qb, (((1,), (1,)), ((), ())), preferred_element_type=F32) + bias
        sp = _softplus(z)
        if diag:
            sp = jnp.where(valid, sp, 0.0)
        cs = _suffix_sum(tri, sp)
        w = jnp.exp(z - cs - c)
        if diag:
            w = jnp.where(valid, w, 0.0)
        vt = vt_scr[kj, hh * hd:(hh + 1) * hd, :]
        pv = jnp.dot(vt, w.astype(BF16), preferred_element_type=F32)
        return pv, c + cs[0:1, :]

    for hh in range(hp):
        bias = bias_ref[g * hp + hh]

        def q_loop(qi, carry, hh=hh, bias=bias):
            pv, c = block(qi, qi, jnp.zeros((1, tq), F32), hh, bias, True)

            def k_loop(i, kc):
                pv_, c_ = kc
                dpv, c_ = block(qi, qi - 1 - i, c_, hh, bias, False)
                return pv_ + dpv, c_

            pv, c = lax.fori_loop(0, qi, k_loop, (pv, c))
            acc_scr[qi, hh * hd:(hh + 1) * hd, :] = pv
            return carry

        lax.fori_loop(0, nb, q_loop, 0)
    for j in range(nb):
        o_ref[j * tq:(j + 1) * tq, :] = acc_scr[j].T.astype(BF16)


def _attn_prompt(q, k, v, bias, tq=256, hp=2):
    b, t, a = q.shape
    w = hp * HEAD_DIM
    blk = lambda bi, gi: (bi, 0, gi)
    spec = pl.BlockSpec((None, t, w), blk)
    return pl.pallas_call(
        functools.partial(_attn_prompt_body, tq=tq, hp=hp),
        grid=(b, a // w),
        in_specs=[pl.BlockSpec(memory_space=pltpu.SMEM), spec, spec, spec],
        out_specs=spec,
        out_shape=jax.ShapeDtypeStruct((b, t, a), BF16),
        scratch_shapes=[pltpu.VMEM((t // tq, w, tq), BF16), pltpu.VMEM((t // tq, w, tq), F32)],
        compiler_params=_params(("parallel", "parallel")),
        name="attn_prompt",
    )(bias, q, k, v)


def _attn_sample_body(pt_ref, bias_ref, q_ref, kn_ref, vn_ref, *rest, pp):
    kpg = rest[:pp]
    vpg = rest[pp:2 * pp]
    o_ref = rest[2 * pp]
    qbd_scr, bcol_scr, c_scr, acc_scr, pad_scr = rest[2 * pp + 1:]
    s = pl.program_id(1)
    t_new, a = q_ref.shape
    hd = HEAD_DIM
    nh = a // hd
    nq = nh * t_new

    def chunk(kt, vt, masked):
        r = kt.shape[1]
        summed = lax.broadcasted_iota(jnp.int32, (r, r), 0)
        at = lax.broadcasted_iota(jnp.int32, (r, r), 1)
        tri = (summed >= at).astype(BF16)
        z = jnp.dot(qbd_scr[...], kt, preferred_element_type=F32) + bcol_scr[...]
        sp = _softplus(z)
        if masked:
            key = lax.broadcasted_iota(jnp.int32, z.shape, 1)
            qry = lax.broadcasted_iota(jnp.int32, z.shape, 0) % t_new
            ok = key < qry
            sp = jnp.where(ok, sp, 0.0)
        hi, lo = _split_bf16(sp)
        cs = (jnp.dot(hi, tri, preferred_element_type=F32)
              + jnp.dot(lo, tri, preferred_element_type=F32))
        w = jnp.exp(z - cs - c_scr[...])
        if masked:
            w = jnp.where(ok, w, 0.0)
        c_scr[...] += cs[:, 0:1]
        acc_scr[...] += lax.dot_general(w.astype(BF16), vt, (((1,), (1,)), ((), ())),
                                        preferred_element_type=F32)

    @pl.when(s == 0)
    def _():
        qt = jnp.concatenate([q_ref[...].astype(F32)] * nh, axis=0)
        hq = lax.broadcasted_iota(jnp.int32, qt.shape, 0)
        col = lax.broadcasted_iota(jnp.int32, qt.shape, 1)
        qbd_scr[...] = jnp.where(hq // t_new == col // hd, qt, 0.0).astype(BF16)
        row = lax.broadcasted_iota(jnp.int32, (nq, 1), 0)
        bcol = jnp.zeros((nq, 1), F32)
        for h in range(nh):
            bcol = jnp.where(row // t_new == h, bias_ref[h], bcol)
        bcol_scr[...] = bcol
        c_scr[...] = jnp.zeros_like(c_scr)
        acc_scr[...] = jnp.zeros_like(acc_scr)
        pad_scr[...] = jnp.zeros_like(pad_scr)
        pad_scr[0:t_new, :] = kn_ref[...]
        knt = pad_scr[...].T.astype(BF16)
        pad_scr[0:t_new, :] = vn_ref[...]
        vnt = pad_scr[...].T.astype(BF16)
        chunk(knt, vnt, True)

    for i in range(0, pp, 2):
        kt = jnp.concatenate([kpg[i + 1][...].astype(BF16), kpg[i][...].astype(BF16)], axis=1)
        vt = jnp.concatenate([vpg[i + 1][...].astype(BF16), vpg[i][...].astype(BF16)], axis=1)
        chunk(kt, vt, False)

    @pl.when(s == pl.num_programs(1) - 1)
    def _():
        col = lax.broadcasted_iota(jnp.int32, (t_new, a), 1)
        out = jnp.zeros((t_new, a), F32)
        for h in range(nh):
            out = out + jnp.where(col // hd == h, acc_scr[h * t_new:(h + 1) * t_new, :], 0.0)
        o_ref[...] = out.astype(BF16)


def _attn_sample(q, kn, vn, cache_kt, cache_vt, layer, page_table, bias, pp=8):
    b, t, a = q.shape
    n_pages = page_table.shape[1]
    nq = (a // HEAD_DIM) * t
    new_spec = pl.BlockSpec((None, t, a), lambda bi, si, pt: (bi, 0, 0))

    def page_spec(i):
        return pl.BlockSpec(
            (None, None, a, PAGE_SIZE),
            lambda bi, si, pt: (layer, pt[bi, n_pages - 1 - (si * pp + i)], 0, 0))

    grid_spec = pltpu.PrefetchScalarGridSpec(
        num_scalar_prefetch=1,
        grid=(b, n_pages // pp),
        in_specs=([pl.BlockSpec(memory_space=pltpu.SMEM), new_spec, new_spec, new_spec]
                  + [page_spec(i) for i in range(pp)] * 2),
        out_specs=new_spec,
        scratch_shapes=[
            pltpu.VMEM((nq, a), BF16),
            pltpu.VMEM((nq, 1), F32),
            pltpu.VMEM((nq, 1), F32),
            pltpu.VMEM((nq, a), F32),
            pltpu.VMEM((PAGE_SIZE, a), F32),
        ],
    )
    return pl.pallas_call(
        functools.partial(_attn_sample_body, pp=pp),
        grid_spec=grid_spec,
        out_shape=jax.ShapeDtypeStruct((b, t, a), BF16),
        compiler_params=_params(("parallel", "arbitrary")),
        name="attn_sample",
    )(page_table, bias, q, kn, vn, *([cache_kt] * pp), *([cache_vt] * pp))


def _mix_body(*refs, has_hist, pos0):
    if has_hist:
        (x_ref, u_ref, p_ref, attn_ref, gate_ref, ust_ref, pst_ref, uh_ref, ph_ref,
         cw_ref, cb_ref, lg_ref, lb_ref, wco_ref, wao_ref, pw_ref, ps_ref, wo_ref,
         o_ref, uext, pext, ycv) = refs
    else:
        (x_ref, u_ref, p_ref, attn_ref, gate_ref, ust_ref, pst_ref,
         cw_ref, cb_ref, lg_ref, lb_ref, wco_ref, wao_ref, pw_ref, ps_ref, wo_ref,
         o_ref, uext, pext, ycv) = refs
    i = pl.program_id(1)
    nb, ts, c = u_ref.shape
    m = nb * ts
    d = x_ref.shape[1]

    if has_hist:
        first = i == 0
        uext[:, 0:CONV_HIST, :] = jnp.where(first, ust_ref[...], uh_ref[...])
        pext[:, 0:POOL_HIST, :] = jnp.where(first, pst_ref[...], ph_ref[...])
    else:
        uext[:, 0:CONV_HIST, :] = ust_ref[...]
        pext[:, 0:POOL_HIST, :] = pst_ref[...]
    uext[:, CONV_HIST:CONV_HIST + ts, :] = u_ref[...]
    pext[:, POOL_HIST:POOL_HIST + ts, :] = p_ref[...]

    off = CONV_HIST - (CONV_W - 1)
    rc = min(ts, 64)
    for lg in range(c // LANES):
        ls = slice(lg * LANES, (lg + 1) * LANES)
        for r0 in range(0, ts, rc):
            acc = jnp.zeros((nb, rc, LANES), F32)
            for j in range(CONV_W):
                acc = acc + cw_ref[j:j + 1, ls] * uext[:, r0 + off + j:r0 + off + j + rc, ls]
            ycv[:, r0:r0 + rc, ls] = acc

    yf = ycv[...].reshape(m, c) + cb_ref[...]
    mu = jnp.mean(yf, axis=-1, keepdims=True)
    yc = yf - mu
    var = jnp.mean(yc * yc, axis=-1, keepdims=True)
    yn = yc * lax.rsqrt(var + EPS) * lg_ref[...] + lb_ref[...]
    y_conv = jnp.dot((yn * jax.nn.sigmoid(yn)).astype(BF16), wco_ref[...],
                     preferred_element_type=F32)

    y_attn = jnp.dot(attn_ref[...], wao_ref[...], preferred_element_type=F32)

    pos = pos0 + i * ts + lax.broadcasted_iota(jnp.int32, (1, ts, 1), 1)
    pooled = []
    for gi, win in enumerate(POOL_WINDOWS):
        ls = slice(gi * LANES, (gi + 1) * LANES)
        cur = pext[:, POOL_HIST:POOL_HIST + ts, ls]
        ws = cur
        for k in range(1, win):
            ws = ws + pext[:, POOL_HIST - k:POOL_HIST - k + ts, ls]
        cnt = jnp.minimum(pos + 1, win).astype(F32)
        pg = (ws / cnt - cur).reshape(m, LANES).astype(BF16)
        pooled.append(jnp.dot(pg, pw_ref[gi], preferred_element_type=F32))
    y_pool = jnp.concatenate(pooled, axis=-1) * ps_ref[...]

    g0 = gate_ref[:, 0:d].astype(F32)
    g1 = gate_ref[:, d:2 * d].astype(F32)
    g2 = gate_ref[:, 2 * d:3 * d].astype(F32)
    merged = g0 * y_conv + g1 * y_attn + g2 * y_pool
    o_ref[...] = x_ref[...] + jnp.dot(merged.astype(BF16), wo_ref[...], preferred_element_type=F32)


def _mix(x, u, p, attn, gate, ust, pst, lw, nb, ts, pos0):
    (conv_w, conv_b, ln_g, ln_b, wco, wao, pool_w, pool_scale, wo) = lw
    b, t, c = u.shape
    d = x.shape[1]
    nt = t // ts
    has_hist = nt > 1
    assert nb == 1 or not has_hist
    tok = lambda bi, ti: (bi * nt + ti, 0)
    seq = lambda bi, ti: (bi, ti, 0)
    in_specs = [
        pl.BlockSpec((nb * ts, d), tok),
        pl.BlockSpec((nb, ts, c), seq),
        pl.BlockSpec((nb, ts, c), seq),
        pl.BlockSpec((nb * ts, c), tok),
        pl.BlockSpec((nb * ts, 3 * d), tok),
        pl.BlockSpec((nb, CONV_HIST, c), lambda bi, ti: (bi, 0, 0)),
        pl.BlockSpec((nb, POOL_HIST, c), lambda bi, ti: (bi, 0, 0)),
    ]
    args = [x, u, p, attn, gate, ust, pst]
    if has_hist:
        in_specs += [
            pl.BlockSpec((nb, CONV_HIST, c),
                         lambda bi, ti: (bi, jnp.maximum(ti * (ts // CONV_HIST) - 1, 0), 0)),
            pl.BlockSpec((nb, POOL_HIST, c),
                         lambda bi, ti: (bi, jnp.maximum(ti * (ts // POOL_HIST) - 1, 0), 0)),
        ]
        args += [u, p]
    consts = [conv_w, conv_b.reshape(1, c), ln_g.reshape(1, c), ln_b.reshape(1, c),
              wco, wao, pool_w, pool_scale.reshape(1, d), wo]
    in_specs += [_resident(a.shape) for a in consts]
    return pl.pallas_call(
        functools.partial(_mix_body, has_hist=has_hist, pos0=pos0),
        grid=(b // nb, nt),
        in_specs=in_specs,
        out_specs=pl.BlockSpec((nb * ts, d), tok),
        out_shape=jax.ShapeDtypeStruct(x.shape, F32),
        scratch_shapes=[pltpu.VMEM((nb, CONV_HIST + ts, c), F32),
                        pltpu.VMEM((nb, POOL_HIST + ts, c), F32),
                        pltpu.VMEM((nb, ts, c), F32)],
        compiler_params=_params(("parallel", "parallel")),
        name="mix",
    )(*args, *consts)


def _swiglu_into(hn, w1_ref, w3_ref, a_scr, fc):
    f = a_scr.shape[1]
    for c0 in range(0, f, fc):
        n = min(fc, f - c0)
        h1 = jnp.dot(hn, w1_ref[:, c0:c0 + n], preferred_element_type=F32)
        h3 = jnp.dot(hn, w3_ref[:, c0:c0 + n], preferred_element_type=F32)
        a_scr[:, c0:c0 + n] = (h1 * jax.nn.sigmoid(h1) * h3).astype(BF16)


def _ffn_body(x_ref, g_ref, w1_ref, w3_ref, w2_ref, o_ref, a_scr, *, fc):
    x = x_ref[...]
    hn = _rms(x, g_ref[...]).astype(BF16)
    _swiglu_into(hn, w1_ref, w3_ref, a_scr, fc)
    o_ref[...] = x + jnp.dot(a_scr[...], w2_ref[...], preferred_element_type=F32)


def _ffn(x, g, w1, w3, w2, tm):
    n, d = x.shape
    f = w1.shape[1]
    row = lambda i: (i, 0)
    return pl.pallas_call(
        functools.partial(_ffn_body, fc=512),
        grid=(n // tm,),
        in_specs=[pl.BlockSpec((tm, d), row), _resident((1, d)),
                  _resident(w1.shape), _resident(w3.shape), _resident(w2.shape)],
        out_specs=pl.BlockSpec((tm, d), row),
        out_shape=jax.ShapeDtypeStruct((n, d), F32),
        scratch_shapes=[pltpu.VMEM((tm, f), BF16)],
        compiler_params=_params(("parallel",)),
        name="ffn",
    )(x, g.reshape(1, d), w1, w3, w2)


def _router_gate(hn_f32, wr_ref, n_exp):
    xh, xl = _split_bf16(hn_f32)
    wh, wl = _split_bf16(wr_ref[...])
    logits = (jnp.dot(xh, wh, preferred_element_type=F32)
              + jnp.dot(xh, wl, preferred_element_type=F32)
              + jnp.dot(xl, wh, preferred_element_type=F32))
    lane = lax.broadcasted_iota(jnp.int32, logits.shape, 1)
    neg = jnp.float32(-jnp.inf)
    l1 = jnp.where(lane < n_exp, logits, neg)
    m1 = jnp.max(l1, axis=-1, keepdims=True)
    i1 = jnp.min(jnp.where(l1 == m1, lane, LANES), axis=-1, keepdims=True)
    l2 = jnp.where(lane == i1, neg, l1)
    m2 = jnp.max(l2, axis=-1, keepdims=True)
    i2 = jnp.min(jnp.where(l2 == m2, lane, LANES), axis=-1, keepdims=True)
    e2 = jnp.exp(m2 - m1)
    p1 = 1.0 / (1.0 + e2)
    p2 = e2 / (1.0 + e2)
    return jnp.where(lane == i1, p1, 0.0) + jnp.where(lane == i2, p2, 0.0)


def _moe_body(x_ref, g_ref, wr_ref, w1_ref, w3_ref, w2_ref, gf_ref, o_ref,
              hn_scr, gate_scr, acc_scr, a_scr, *, fc, n_exp, final_norm):
    e = pl.program_id(1)

    @pl.when(e == 0)
    def _():
        hn = _rms(x_ref[...], g_ref[...])
        hn_scr[...] = hn.astype(BF16)
        gate_scr[...] = _router_gate(hn, wr_ref, n_exp)
        acc_scr[...] = jnp.zeros_like(acc_scr)

    _swiglu_into(hn_scr[...], w1_ref, w3_ref, a_scr, fc)
    y = jnp.dot(a_scr[...], w2_ref[...], preferred_element_type=F32)
    lane = lax.broadcasted_iota(jnp.int32, gate_scr.shape, 1)
    ge = jnp.sum(jnp.where(lane == e, gate_scr[...], 0.0), axis=-1, keepdims=True)
    acc_scr[...] += ge * y

    @pl.when(e == n_exp - 1)
    def _():
        out = x_ref[...] + acc_scr[...]
        if final_norm:
            out = _rms(out, gf_ref[...])
        o_ref[...] = out


def _moe(x, g, wr, w1, w3, w2, gf, tm, final_norm):
    n, d = x.shape
    n_exp, _, f = w1.shape
    wr_pad = jnp.zeros((d, LANES), F32).at[:, :n_exp].set(wr)
    row = lambda i, e: (i, 0)
    ex = lambda i, e: (e, 0, 0)
    return pl.pallas_call(
        functools.partial(_moe_body, fc=512, n_exp=n_exp, final_norm=final_norm),
        grid=(n // tm, n_exp),
        in_specs=[pl.BlockSpec((tm, d), row), _resident((1, d)), _resident((d, LANES)),
                  pl.BlockSpec((None, d, f), ex), pl.BlockSpec((None, d, f), ex),
                  pl.BlockSpec((None, f, d), ex), _resident((1, d))],
        out_specs=pl.BlockSpec((tm, d), row),
        out_shape=jax.ShapeDtypeStruct((n, d), F32),
        scratch_shapes=[pltpu.VMEM((tm, d), BF16), pltpu.VMEM((tm, LANES), F32),
                        pltpu.VMEM((tm, d), F32), pltpu.VMEM((tm, f), BF16)],
        compiler_params=_params(("parallel", "arbitrary")),
        name="moe",
    )(x, g.reshape(1, d), wr_pad, w1, w3, w2, gf.reshape(1, d))


def _norm_body(x_ref, g_ref, o_ref):
    o_ref[...] = _rms(x_ref[...], g_ref[...])


def _final_norm(x, g, tm):
    n, d = x.shape
    row = lambda i: (i, 0)
    return pl.pallas_call(
        _norm_body,
        grid=(n // tm,),
        in_specs=[pl.BlockSpec((tm, d), row), _resident((1, d))],
        out_specs=pl.BlockSpec((tm, d), row),
        out_shape=jax.ShapeDtypeStruct((n, d), F32),
        compiler_params=_params(("parallel",)),
        name="final_norm",
    )(x, g.reshape(1, d))


def _pad_front(state, rows):
    b, r, c = state.shape
    return jnp.concatenate([jnp.zeros((b, rows - r, c), state.dtype), state], axis=1)


def kernel(x_prompt, x_sample, cache_k, cache_v, state_conv, state_pool, page_table, norm_mix, w_in, conv_w, conv_b, conv_ln_g, conv_ln_b, w_conv_out, w_attn_out, sb_bias, pool_w, pool_scale, w_o, norm_ffn, ffn_w1, ffn_w3, ffn_w2, moe_router, moe_w1, moe_w3, moe_w2, norm_final):
    bp, tp, d = x_prompt.shape
    bs, tsm, _ = x_sample.shape
    depth = w_in.shape[0]
    a = N_HEADS * HEAD_DIM
    n_pages = page_table.shape[1]
    past = n_pages * PAGE_SIZE
    cshape = (cache_k.shape[0], cache_k.shape[1], a, PAGE_SIZE)
    ck = jnp.transpose(cache_k, (0, 1, 3, 4, 2)).reshape(cshape)
    cv = jnp.transpose(cache_v, (0, 1, 3, 4, 2)).reshape(cshape)

    xp = x_prompt.reshape(bp * tp, d)
    xs = x_sample.reshape(bs * tsm, d)
    outs = {k: [] for k in ("kp", "vp", "cp", "pp", "ks", "vs", "cs", "ps")}
    for l in range(depth):
        w_in_l = w_in[l].astype(BF16)
        lw = (conv_w[l], conv_b[l], conv_ln_g[l], conv_ln_b[l], w_conv_out[l].astype(BF16),
              w_attn_out[l].astype(BF16), pool_w[l].astype(BF16), pool_scale[l], w_o[l].astype(BF16))

        u, q, k, v, p, gate = _in_proj(xp, norm_mix[l], w_in_l, tm=512)
        c = u.shape[1]
        attn = _attn_prompt(q.reshape(bp, tp, a), k.reshape(bp, tp, a), v.reshape(bp, tp, a),
                            sb_bias[l])
        u3 = u.reshape(bp, tp, c)
        p3 = p.reshape(bp, tp, c)
        xp = _mix(xp, u3, p3, attn.reshape(bp * tp, a), gate,
                  jnp.zeros((bp, CONV_HIST, c), F32), jnp.zeros((bp, POOL_HIST, c), F32),
                  lw, nb=1, ts=256, pos0=0)
        outs["kp"].append(k.reshape(bp, tp, N_HEADS, HEAD_DIM))
        outs["vp"].append(v.reshape(bp, tp, N_HEADS, HEAD_DIM))
        outs["cp"].append(u3[:, tp - (CONV_W - 1):])
        outs["pp"].append(p3[:, tp - (POOL_MAX - 1):])

        u, q, k, v, p, gate = _in_proj(xs, norm_mix[l], w_in_l, tm=bs * tsm)
        attn = _attn_sample(q.reshape(bs, tsm, a), k.reshape(bs, tsm, a), v.reshape(bs, tsm, a),
                            ck, cv, l, page_table, sb_bias[l])
        u3 = u.reshape(bs, tsm, c)
        p3 = p.reshape(bs, tsm, c)
        xs = _mix(xs, u3, p3, attn.reshape(bs * tsm, a), gate,
                  _pad_front(state_conv[l], CONV_HIST), _pad_front(state_pool[l], POOL_HIST),
                  lw, nb=bs, ts=tsm, pos0=past)
        outs["ks"].append(k.reshape(bs, tsm, N_HEADS, HEAD_DIM))
        outs["vs"].append(v.reshape(bs, tsm, N_HEADS, HEAD_DIM))
        outs["cs"].append(jnp.concatenate([state_conv[l], u3], axis=1)[:, -(CONV_W - 1):])
        outs["ps"].append(jnp.concatenate([state_pool[l], p3], axis=1)[:, -(POOL_MAX - 1):])

        i = l // 2
        last = l == depth - 1
        if l % 2 == 0:
            w1, w3, w2 = ffn_w1[i].astype(BF16), ffn_w3[i].astype(BF16), ffn_w2[i].astype(BF16)
            xp = _ffn(xp, norm_ffn[l], w1, w3, w2, tm=512)
            xs = _ffn(xs, norm_ffn[l], w1, w3, w2, tm=bs * tsm)
            if last:
                xp = _final_norm(xp, norm_final, tm=512)
                xs = _final_norm(xs, norm_final, tm=bs * tsm)
        else:
            w1, w3, w2 = moe_w1[i].astype(BF16), moe_w3[i].astype(BF16), moe_w2[i].astype(BF16)
            xp = _moe(xp, norm_ffn[l], moe_router[i], w1, w3, w2, norm_final, 512, last)
            xs = _moe(xs, norm_ffn[l], moe_router[i], w1, w3, w2, norm_final, bs * tsm, last)

    return (xp.reshape(bp, tp, d), xs.reshape(bs, tsm, d),
            jnp.stack(outs["kp"]), jnp.stack(outs["vp"]), jnp.stack(outs["cp"]), jnp.stack(outs["pp"]),
            jnp.stack(outs["ks"]), jnp.stack(outs["vs"]), jnp.stack(outs["cs"]), jnp.stack(outs["ps"]))
```

```python
import functools

import jax
import jax.numpy as jnp
from jax import lax
from jax.experimental import pallas as pl
from jax.experimental.pallas import tpu as pltpu

F32 = jnp.float32
BF16 = jnp.bfloat16
EPS = 1e-6
LOG2E = 1.4426950408889634

CONV_W = 31
POOL_WINDOWS = (2, 4, 8, 16)
POOL_MAX = 16
N_HEADS = 8
HEAD_DIM = 64
PAGE_SIZE = 128
TOP_K = 2
CONV_HIST = 32
POOL_HIST = 16
LANES = 128
VMEM_LIMIT = 56 * 1024 * 1024


def _params(sem):
    return pltpu.CompilerParams(dimension_semantics=sem, vmem_limit_bytes=VMEM_LIMIT)


def _resident(shape):
    zeros = (0,) * len(shape)
    return pl.BlockSpec(shape, lambda *_: zeros, pipeline_mode=pl.Buffered(1))


def _rms(x, g):
    return x * lax.rsqrt(jnp.mean(x * x, axis=-1, keepdims=True) + EPS) * g


def _softplus(z):
    return jnp.maximum(z, 0.0) + jnp.log(1.0 + jnp.exp2(jnp.abs(z) * -LOG2E))


def _split_bf16(x):
    hi = x.astype(BF16)
    return hi, (x - hi.astype(F32)).astype(BF16)


def _in_proj_body(x_ref, g_ref, w_ref, u_ref, q_ref, k_ref, v_ref, p_ref, gate_ref, *, q_scale):
    hn = _rms(x_ref[...], g_ref[...]).astype(BF16)
    cd = u_ref.shape[1]
    ad = q_ref.shape[1]
    pd = p_ref.shape[1]

    def proj(c0, n):
        return jnp.dot(hn, w_ref[:, c0:c0 + n], preferred_element_type=F32)

    c = 0
    a = proj(c, cd)
    b = proj(c + cd, cd)
    u_ref[...] = a * jax.nn.sigmoid(b)
    c += 2 * cd
    q_ref[...] = (proj(c, ad) * q_scale).astype(BF16)
    c += ad
    k_ref[...] = proj(c, ad)
    c += ad
    v_ref[...] = proj(c, ad)
    c += ad
    p_ref[...] = proj(c, pd)
    c += pd
    ng = gate_ref.shape[1]
    step = 512
    for j in range(0, ng, step):
        gate_ref[:, j:j + step] = jax.nn.sigmoid(proj(c + j, step)).astype(BF16)


def _in_proj(x, g, w, tm):
    n, d = x.shape
    assert n % tm == 0
    cd = ad = pd = 512
    ng = w.shape[1] - 2 * cd - 3 * ad - pd
    row = lambda i: (i, 0)
    outs = [
        jax.ShapeDtypeStruct((n, cd), F32),
        jax.ShapeDtypeStruct((n, ad), BF16),
        jax.ShapeDtypeStruct((n, ad), F32),
        jax.ShapeDtypeStruct((n, ad), F32),
        jax.ShapeDtypeStruct((n, pd), F32),
        jax.ShapeDtypeStruct((n, ng), BF16),
    ]
    return pl.pallas_call(
        functools.partial(_in_proj_body, q_scale=HEAD_DIM ** -0.5),
        grid=(n // tm,),
        in_specs=[pl.BlockSpec((tm, d), row), _resident((1, d)), _resident(w.shape)],
        out_specs=[pl.BlockSpec((tm, o.shape[1]), row) for o in outs],
        out_shape=outs,
        compiler_params=_params(("parallel",)),
        name="in_proj",
    )(x, g.reshape(1, d), w)


def _attn_prompt_body(bias_ref, q_ref, k_ref, v_ref, o_ref, vt_scr, acc_scr, *, tq, hp):
    t_len = q_ref.shape[0]
    nb = t_len // tq
    hd = HEAD_DIM
    g = pl.program_id(1)
    for j in range(nb):
        vt_scr[j] = v_ref[j * tq:(j + 1) * tq, :].T.astype(BF16)
    key = lax.broadcasted_iota(jnp.int32, (tq, tq), 0)
    qry = lax.broadcasted_iota(jnp.int32, (tq, tq), 1)
    valid = key < qry
    tri = (qry >= key).astype(BF16)

    biases = [bias_ref[g * hp + hh] for hh in range(hp)]

    def block(qi, kj, cs_in, diag):
        q_rows = pl.ds(pl.multiple_of(qi * tq, tq), tq)
        k_rows = pl.ds(pl.multiple_of(kj * tq, tq), tq)
        heads = [slice(hh * hd, (hh + 1) * hd) for hh in range(hp)]
        zs = [lax.dot_general(k_ref[k_rows, hs].astype(BF16), q_ref[q_rows, hs],
                              (((1,), (1,)), ((), ())), preferred_element_type=F32) + biases[hh]
              for hh, hs in enumerate(heads)]
        sps = [_softplus(z) for z in zs]
        if diag:
            sps = [jnp.where(valid, sp, 0.0) for sp in sps]
        css = [jnp.dot(tri, sp.astype(BF16), preferred_element_type=F32) for sp in sps]
        ws = [jnp.exp(z - cs - c) for z, cs, c in zip(zs, css, cs_in)]
        if diag:
            ws = [jnp.where(valid, w, 0.0) for w in ws]
        pvs = [jnp.dot(vt_scr[kj, hs, :], w.astype(BF16), preferred_element_type=F32)
               for hs, w in zip(heads, ws)]
        for hs, pv in zip(heads, pvs):
            if diag:
                acc_scr[qi, hs, :] = pv
            else:
                acc_scr[qi, hs, :] += pv
        return tuple(c + cs[0:1, :] for c, cs in zip(cs_in, css))

    def q_loop(qi, carry):
        c0 = block(qi, qi, (jnp.zeros((1, tq), F32),) * hp, True)
        lax.fori_loop(0, qi, lambda i, c: block(qi, qi - 1 - i, c, False), c0)
        return carry

    lax.fori_loop(0, nb, q_loop, 0)
    for j in range(nb):
        o_ref[j * tq:(j + 1) * tq, :] = acc_scr[j].T.astype(BF16)


def _attn_prompt(q, k, v, bias, tq=256, hp=4):
    b, t, a = q.shape
    w = hp * HEAD_DIM
    assert t % tq == 0 and a % w == 0
    blk = lambda bi, gi: (bi, 0, gi)
    spec = pl.BlockSpec((None, t, w), blk)
    return pl.pallas_call(
        functools.partial(_attn_prompt_body, tq=tq, hp=hp),
        grid=(b, a // w),
        in_specs=[pl.BlockSpec(memory_space=pltpu.SMEM), spec, spec, spec],
        out_specs=spec,
        out_shape=jax.ShapeDtypeStruct((b, t, a), BF16),
        scratch_shapes=[pltpu.VMEM((t // tq, w, tq), BF16), pltpu.VMEM((t // tq, w, tq), F32)],
        compiler_params=_params(("parallel", "parallel")),
        name="attn_prompt",
    )(bias, q, k, v)


def _attn_sample_body(pt_ref, bias_ref, q_ref, kn_ref, vn_ref, *rest, pp):
    kpg = rest[:pp]
    vpg = rest[pp:2 * pp]
    o_ref = rest[2 * pp]
    qbd_scr, bcol_scr, c_scr, acc_scr, pad_scr = rest[2 * pp + 1:]
    s = pl.program_id(1)
    t_new, a = q_ref.shape
    hd = HEAD_DIM
    nh = a // hd
    nq = nh * t_new

    def chunk(kt, vt, masked):
        r = kt.shape[1]
        sb = min(r, 2 * PAGE_SIZE)
        nblk = r // sb
        summed = lax.broadcasted_iota(jnp.int32, (sb, sb), 0)
        at = lax.broadcasted_iota(jnp.int32, (sb, sb), 1)
        tri = (summed >= at).astype(BF16)
        z = jnp.dot(qbd_scr[...], kt, preferred_element_type=F32) + bcol_scr[...]
        sp = _softplus(z)
        if masked:
            key = lax.broadcasted_iota(jnp.int32, z.shape, 1)
            qry = lax.broadcasted_iota(jnp.int32, z.shape, 0) % t_new
            ok = key < qry
            sp = jnp.where(ok, sp, 0.0)
        sp_st = jnp.concatenate([sp[:, j * sb:(j + 1) * sb] for j in range(nblk)], axis=0)
        cs_st = jnp.dot(sp_st.astype(BF16), tri, preferred_element_type=F32)
        run = c_scr[...]
        ws = [None] * nblk
        for j in reversed(range(nblk)):
            cs = cs_st[j * nq:(j + 1) * nq, :]
            ws[j] = jnp.exp(z[:, j * sb:(j + 1) * sb] - cs - run)
            run = run + cs[:, 0:1]
        c_scr[...] = run
        w = jnp.concatenate(ws, axis=1)
        if masked:
            w = jnp.where(ok, w, 0.0)
        acc_scr[...] += lax.dot_general(w.astype(BF16), vt, (((1,), (1,)), ((), ())),
                                        preferred_element_type=F32)

    @pl.when(s == 0)
    def _():
        qt = jnp.concatenate([q_ref[...].astype(F32)] * nh, axis=0)
        hq = lax.broadcasted_iota(jnp.int32, qt.shape, 0)
        col = lax.broadcasted_iota(jnp.int32, qt.shape, 1)
        qbd_scr[...] = jnp.where(hq // t_new == col // hd, qt, 0.0).astype(BF16)
        row = lax.broadcasted_iota(jnp.int32, (nq, 1), 0)
        bcol = jnp.zeros((nq, 1), F32)
        for h in range(nh):
            bcol = jnp.where(row // t_new == h, bias_ref[h], bcol)
        bcol_scr[...] = bcol
        c_scr[...] = jnp.zeros_like(c_scr)
        acc_scr[...] = jnp.zeros_like(acc_scr)
        pad_scr[...] = jnp.zeros_like(pad_scr)
        pad_scr[0:t_new, :] = kn_ref[...]
        knt = pad_scr[...].T.astype(BF16)
        pad_scr[0:t_new, :] = vn_ref[...]
        vnt = pad_scr[...].T.astype(BF16)
        chunk(knt, vnt, True)

    kt = jnp.concatenate([kpg[i][...].astype(BF16) for i in reversed(range(pp))], axis=1)
    vt = jnp.concatenate([vpg[i][...].astype(BF16) for i in reversed(range(pp))], axis=1)
    chunk(kt, vt, False)

    @pl.when(s == pl.num_programs(1) - 1)
    def _():
        col = lax.broadcasted_iota(jnp.int32, (t_new, a), 1)
        out = jnp.zeros((t_new, a), F32)
        for h in range(nh):
            out = out + jnp.where(col // hd == h, acc_scr[h * t_new:(h + 1) * t_new, :], 0.0)
        o_ref[...] = out.astype(BF16)


def _attn_sample(q, kn, vn, cache_kt, cache_vt, layer, page_table, bias, pp=16):
    b, t, a = q.shape
    n_pages = page_table.shape[1]
    assert n_pages % pp == 0 and t <= PAGE_SIZE
    nq = (a // HEAD_DIM) * t
    new_spec = pl.BlockSpec((None, t, a), lambda bi, si, pt: (bi, 0, 0))

    def page_spec(i):
        return pl.BlockSpec(
            (None, None, a, PAGE_SIZE),
            lambda bi, si, pt: (layer, pt[bi, n_pages - 1 - (si * pp + i)], 0, 0))

    grid_spec = pltpu.PrefetchScalarGridSpec(
        num_scalar_prefetch=1,
        grid=(b, n_pages // pp),
        in_specs=([pl.BlockSpec(memory_space=pltpu.SMEM), new_spec, new_spec, new_spec]
                  + [page_spec(i) for i in range(pp)] * 2),
        out_specs=new_spec,
        scratch_shapes=[
            pltpu.VMEM((nq, a), BF16),
            pltpu.VMEM((nq, 1), F32),
            pltpu.VMEM((nq, 1), F32),
            pltpu.VMEM((nq, a), F32),
            pltpu.VMEM((PAGE_SIZE, a), F32),
        ],
    )
    return pl.pallas_call(
        functools.partial(_attn_sample_body, pp=pp),
        grid_spec=grid_spec,
        out_shape=jax.ShapeDtypeStruct((b, t, a), BF16),
        compiler_params=_params(("parallel", "arbitrary")),
        name="attn_sample",
    )(page_table, bias, q, kn, vn, *([cache_kt] * pp), *([cache_vt] * pp))


def _mix_body(*refs, has_hist, pos0):
    if has_hist:
        (x_ref, u_ref, p_ref, attn_ref, gate_ref, ust_ref, pst_ref, uh_ref, ph_ref,
         cw_ref, cb_ref, lg_ref, lb_ref, wco_ref, wao_ref, pw_ref, ps_ref, wo_ref,
         o_ref, uext, pext, ycv) = refs
    else:
        (x_ref, u_ref, p_ref, attn_ref, gate_ref, ust_ref, pst_ref,
         cw_ref, cb_ref, lg_ref, lb_ref, wco_ref, wao_ref, pw_ref, ps_ref, wo_ref,
         o_ref, uext, pext, ycv) = refs
    i = pl.program_id(1)
    nb, ts, c = u_ref.shape
    m = nb * ts
    d = x_ref.shape[1]

    if has_hist:
        first = i == 0
        uext[:, 0:CONV_HIST, :] = jnp.where(first, ust_ref[...], uh_ref[...])
        pext[:, 0:POOL_HIST, :] = jnp.where(first, pst_ref[...], ph_ref[...])
    else:
        uext[:, 0:CONV_HIST, :] = ust_ref[...]
        pext[:, 0:POOL_HIST, :] = pst_ref[...]
    uext[:, CONV_HIST:CONV_HIST + ts, :] = u_ref[...]
    pext[:, POOL_HIST:POOL_HIST + ts, :] = p_ref[...]

    off = CONV_HIST - (CONV_W - 1)
    rc = min(ts, 64)
    for lg in range(c // LANES):
        ls = slice(lg * LANES, (lg + 1) * LANES)
        for r0 in range(0, ts, rc):
            acc = jnp.zeros((nb, rc, LANES), F32)
            for j in range(CONV_W):
                acc = acc + cw_ref[j:j + 1, ls] * uext[:, r0 + off + j:r0 + off + j + rc, ls]
            ycv[:, r0:r0 + rc, ls] = acc

    yf = ycv[...].reshape(m, c) + cb_ref[...]
    mu = jnp.mean(yf, axis=-1, keepdims=True)
    yc = yf - mu
    var = jnp.mean(yc * yc, axis=-1, keepdims=True)
    yn = yc * lax.rsqrt(var + EPS) * lg_ref[...] + lb_ref[...]
    y_conv = jnp.dot((yn * jax.nn.sigmoid(yn)).astype(BF16), wco_ref[...],
                     preferred_element_type=F32)

    y_attn = jnp.dot(attn_ref[...], wao_ref[...], preferred_element_type=F32)

    pos = pos0 + i * ts + lax.broadcasted_iota(jnp.int32, (1, ts, 1), 1)
    pooled = []
    for gi, win in enumerate(POOL_WINDOWS):
        ls = slice(gi * LANES, (gi + 1) * LANES)
        cur = pext[:, POOL_HIST:POOL_HIST + ts, ls]
        ws = cur
        for k in range(1, win):
            ws = ws + pext[:, POOL_HIST - k:POOL_HIST - k + ts, ls]
        cnt = jnp.minimum(pos + 1, win).astype(F32)
        pg = (ws / cnt - cur).reshape(m, LANES).astype(BF16)
        pooled.append(jnp.dot(pg, pw_ref[gi], preferred_element_type=F32))
    y_pool = jnp.concatenate(pooled, axis=-1) * ps_ref[...]

    g0 = gate_ref[:, 0:d].astype(F32)
    g1 = gate_ref[:, d:2 * d].astype(F32)
    g2 = gate_ref[:, 2 * d:3 * d].astype(F32)
    merged = g0 * y_conv + g1 * y_attn + g2 * y_pool
    o_ref[...] = x_ref[...] + jnp.dot(merged.astype(BF16), wo_ref[...], preferred_element_type=F32)


def _mix(x, u, p, attn, gate, ust, pst, lw, nb, ts, pos0):
    (conv_w, conv_b, ln_g, ln_b, wco, wao, pool_w, pool_scale, wo) = lw
    b, t, c = u.shape
    d = x.shape[1]
    nt = t // ts
    has_hist = nt > 1
    assert t % ts == 0 and b % nb == 0 and (nb == 1 or not has_hist)
    tok = lambda bi, ti: (bi * nt + ti, 0)
    seq = lambda bi, ti: (bi, ti, 0)
    in_specs = [
        pl.BlockSpec((nb * ts, d), tok),
        pl.BlockSpec((nb, ts, c), seq),
        pl.BlockSpec((nb, ts, c), seq),
        pl.BlockSpec((nb * ts, c), tok),
        pl.BlockSpec((nb * ts, 3 * d), tok),
        pl.BlockSpec((nb, CONV_HIST, c), lambda bi, ti: (bi, 0, 0)),
        pl.BlockSpec((nb, POOL_HIST, c), lambda bi, ti: (bi, 0, 0)),
    ]
    args = [x, u, p, attn, gate, ust, pst]
    if has_hist:
        in_specs += [
            pl.BlockSpec((nb, CONV_HIST, c),
                         lambda bi, ti: (bi, jnp.maximum(ti * (ts // CONV_HIST) - 1, 0), 0)),
            pl.BlockSpec((nb, POOL_HIST, c),
                         lambda bi, ti: (bi, jnp.maximum(ti * (ts // POOL_HIST) - 1, 0), 0)),
        ]
        args += [u, p]
    consts = [conv_w, conv_b.reshape(1, c), ln_g.reshape(1, c), ln_b.reshape(1, c),
              wco, wao, pool_w, pool_scale.reshape(1, d), wo]
    in_specs += [_resident(a.shape) for a in consts]
    return pl.pallas_call(
        functools.partial(_mix_body, has_hist=has_hist, pos0=pos0),
        grid=(b // nb, nt),
        in_specs=in_specs,
        out_specs=pl.BlockSpec((nb * ts, d), tok),
        out_shape=jax.ShapeDtypeStruct(x.shape, F32),
        scratch_shapes=[pltpu.VMEM((nb, CONV_HIST + ts, c), F32),
                        pltpu.VMEM((nb, POOL_HIST + ts, c), F32),
                        pltpu.VMEM((nb, ts, c), F32)],
        compiler_params=_params(("parallel", "parallel")),
        name="mix",
    )(*args, *consts)


def _swiglu_into(hn, w1_ref, w3_ref, a_scr, fc):
    f = a_scr.shape[1]
    for c0 in range(0, f, fc):
        n = min(fc, f - c0)
        h1 = jnp.dot(hn, w1_ref[:, c0:c0 + n], preferred_element_type=F32)
        h3 = jnp.dot(hn, w3_ref[:, c0:c0 + n], preferred_element_type=F32)
        a_scr[:, c0:c0 + n] = (h1 * jax.nn.sigmoid(h1) * h3).astype(BF16)


def _ffn_body(x_ref, g_ref, w1_ref, w3_ref, w2_ref, o_ref, a_scr, *, fc):
    x = x_ref[...]
    hn = _rms(x, g_ref[...]).astype(BF16)
    _swiglu_into(hn, w1_ref, w3_ref, a_scr, fc)
    o_ref[...] = x + jnp.dot(a_scr[...], w2_ref[...], preferred_element_type=F32)


def _ffn(x, g, w1, w3, w2, tm):
    n, d = x.shape
    assert n % tm == 0
    f = w1.shape[1]
    row = lambda i: (i, 0)
    return pl.pallas_call(
        functools.partial(_ffn_body, fc=512),
        grid=(n // tm,),
        in_specs=[pl.BlockSpec((tm, d), row), _resident((1, d)),
                  _resident(w1.shape), _resident(w3.shape), _resident(w2.shape)],
        out_specs=pl.BlockSpec((tm, d), row),
        out_shape=jax.ShapeDtypeStruct((n, d), F32),
        scratch_shapes=[pltpu.VMEM((tm, f), BF16)],
        compiler_params=_params(("parallel",)),
        name="ffn",
    )(x, g.reshape(1, d), w1, w3, w2)


MOE_CHUNK = 1280
MOE_TILE = 256
M_E1, M_E2, M_POS1, M_POS2, M_P1, M_P2 = range(6)
M_FIELDS = 8


def _route_top2(hn_f32, wr_ref, n_exp):
    xh, xl = _split_bf16(hn_f32)
    wh, wl = _split_bf16(wr_ref[...])
    logits = (jnp.dot(xh, wh, preferred_element_type=F32)
              + jnp.dot(xh, wl, preferred_element_type=F32)
              + jnp.dot(xl, wh, preferred_element_type=F32))
    lane = lax.broadcasted_iota(jnp.int32, logits.shape, 1)
    neg = jnp.float32(-jnp.inf)
    l1 = jnp.where(lane < n_exp, logits, neg)
    m1 = jnp.max(l1, axis=-1, keepdims=True)
    i1 = jnp.min(jnp.where(l1 == m1, lane, LANES), axis=-1, keepdims=True)
    l2 = jnp.where(lane == i1, neg, l1)
    m2 = jnp.max(l2, axis=-1, keepdims=True)
    i2 = jnp.min(jnp.where(l2 == m2, lane, LANES), axis=-1, keepdims=True)
    e2 = jnp.exp(m2 - m1)
    return i1, i2, 1.0 / (1.0 + e2), e2 / (1.0 + e2)


def _route_body(x_ref, g_ref, wr_ref, hn_ref, meta_ref, metat_ref, cnt_ref, run_scr, *, n_exp, sub):
    @pl.when(pl.program_id(0) == 0)
    def _():
        run_scr[...] = jnp.zeros_like(run_scr)

    hn = _rms(x_ref[...], g_ref[...])
    hn_ref[...] = hn.astype(BF16)
    i1, i2, p1, p2 = _route_top2(hn, wr_ref, n_exp)
    m = hn.shape[0]
    lane = lax.broadcasted_iota(jnp.int32, (m, LANES), 1)
    sel1 = lane == i1
    sel2 = lane == i2
    assign = (sel1 | sel2).astype(BF16)
    row = lax.broadcasted_iota(jnp.int32, (sub, sub), 0)
    col = lax.broadcasted_iota(jnp.int32, (sub, sub), 1)
    before = (col < row).astype(BF16)
    run = run_scr[0:1, :]
    ranks = []
    for r0 in range(0, m, sub):
        a = assign[r0:r0 + sub, :]
        ranks.append(jnp.dot(before, a, preferred_element_type=F32) + run)
        run = run + jnp.sum(a.astype(F32), axis=0, keepdims=True)
    rank = jnp.concatenate(ranks, axis=0)
    run_scr[...] = jnp.broadcast_to(run, run_scr.shape)
    cnt_ref[...] = jnp.broadcast_to(run, cnt_ref.shape)
    pos1 = jnp.sum(jnp.where(sel1, rank, 0.0), axis=-1, keepdims=True)
    pos2 = jnp.sum(jnp.where(sel2, rank, 0.0), axis=-1, keepdims=True)
    fields = {M_E1: i1.astype(F32), M_E2: i2.astype(F32), M_POS1: pos1, M_POS2: pos2,
              M_P1: p1, M_P2: p2}
    meta = jnp.zeros((m, LANES), F32)
    for j, val in fields.items():
        meta = jnp.where(lane == j, val, meta)
    meta_ref[...] = meta
    metat_ref[...] = meta.T[0:M_FIELDS, :]


def _route(x, g, wr, chunk):
    n, d = x.shape
    n_exp = wr.shape[1]
    wr_pad = jnp.zeros((d, LANES), F32).at[:, :n_exp].set(wr)
    assert n % chunk == 0 and chunk % 256 == 0
    nc = n // chunk
    row = lambda c: (c, 0)
    return pl.pallas_call(
        functools.partial(_route_body, n_exp=n_exp, sub=256),
        grid=(nc,),
        in_specs=[pl.BlockSpec((chunk, d), row), _resident((1, d)), _resident((d, LANES))],
        out_specs=[pl.BlockSpec((chunk, d), row), pl.BlockSpec((chunk, LANES), row),
                   pl.BlockSpec((M_FIELDS, chunk), lambda c: (0, c)),
                   pl.BlockSpec((None, 8, LANES), lambda c: (c, 0, 0))],
        out_shape=[jax.ShapeDtypeStruct((n, d), BF16),
                   jax.ShapeDtypeStruct((n, LANES), F32),
                   jax.ShapeDtypeStruct((M_FIELDS, n), F32),
                   jax.ShapeDtypeStruct((nc, 8, LANES), F32)],
        scratch_shapes=[pltpu.VMEM((8, LANES), F32)],
        compiler_params=_params(("arbitrary",)),
        name="moe_route",
    )(x, g.reshape(1, d), wr_pad)


def _gather_body(ptile, pchunk, pfirst, pvalid, texp, trank0, hn_ref, metat_ref, o_ref, acc_scr):
    i = pl.program_id(0)

    @pl.when(pvalid[i] == 1)
    def _():
        r = ptile[i]
        e = texp[r]
        mt = metat_ref[...]
        e1 = mt[M_E1:M_E1 + 1, :].astype(jnp.int32)
        e2 = mt[M_E2:M_E2 + 1, :].astype(jnp.int32)
        pos = jnp.where(e1 == e, mt[M_POS1:M_POS1 + 1, :],
                        jnp.where(e2 == e, mt[M_POS2:M_POS2 + 1, :], -1.0))
        slot = pos.astype(jnp.int32) - trank0[r]
        rows = lax.broadcasted_iota(jnp.int32, (o_ref.shape[0], slot.shape[1]), 0)
        onehot = (rows == slot).astype(BF16)
        res = jnp.dot(onehot, hn_ref[...], preferred_element_type=F32)

        @pl.when(pfirst[i] == 1)
        def _():
            acc_scr[...] = res

        @pl.when(pfirst[i] == 0)
        def _():
            acc_scr[...] += res

        o_ref[...] = acc_scr[...].astype(BF16)


def _expert_body(texp, nvalid, xs_ref, w1_ref, w3_ref, w2_ref, y_ref, a_scr, *, fc):
    @pl.when(pl.program_id(0) < nvalid[0])
    def _():
        _swiglu_into(xs_ref[...], w1_ref, w3_ref, a_scr, fc)
        y_ref[...] = jnp.dot(a_scr[...], w2_ref[...], preferred_element_type=F32).astype(BF16)


def _combine_body(ptile, pchunk, pfirst, plast, pvalid, texp, trank0,
                  x_ref, meta_ref, y_ref, gf_ref, o_ref, acc_scr, *, final_norm):
    i = pl.program_id(0)

    @pl.when(pvalid[i] == 1)
    def _():
        r = ptile[i]
        e = texp[r]
        m = meta_ref[...]
        is1 = m[:, M_E1:M_E1 + 1].astype(jnp.int32) == e
        is2 = m[:, M_E2:M_E2 + 1].astype(jnp.int32) == e
        pos = jnp.where(is1, m[:, M_POS1:M_POS1 + 1],
                        jnp.where(is2, m[:, M_POS2:M_POS2 + 1], -1.0))
        gate = jnp.where(is1, m[:, M_P1:M_P1 + 1], jnp.where(is2, m[:, M_P2:M_P2 + 1], 0.0))
        slot = pos.astype(jnp.int32) - trank0[r]

        @pl.when(pfirst[i] == 1)
        def _():
            acc_scr[...] = jnp.zeros_like(acc_scr)

        rb = 256
        cols = lax.broadcasted_iota(jnp.int32, (rb, y_ref.shape[0]), 1)
        for r0 in range(0, m.shape[0], rb):
            onehot = (cols == slot[r0:r0 + rb, :]).astype(BF16)
            acc_scr[r0:r0 + rb, :] += gate[r0:r0 + rb, :] * jnp.dot(
                onehot, y_ref[...], preferred_element_type=F32)

        @pl.when(plast[i] == 1)
        def _():
            out = x_ref[...] + acc_scr[...]
            if final_norm:
                out = _rms(out, gf_ref[...])
            o_ref[...] = out


def _pair_list(hit, npairs):
    nb = hit.shape[1]
    idx = jnp.nonzero(hit.reshape(-1), size=npairs, fill_value=-1)[0].astype(jnp.int32)
    valid = idx >= 0
    count = jnp.sum(valid.astype(jnp.int32))
    idx = jnp.where(valid, idx, idx[jnp.maximum(count - 1, 0)])
    a = idx // nb
    b = idx % nb
    off = jnp.full((1,), -1, jnp.int32)
    first = valid & (a != jnp.concatenate([off, a[:-1]]))
    nxt_valid = jnp.concatenate([valid[1:], jnp.zeros((1,), bool)])
    last = valid & ((a != jnp.concatenate([a[1:], off])) | ~nxt_valid)
    i32 = lambda v: v.astype(jnp.int32)
    return a, b, i32(first), i32(last), i32(valid)


def _moe(x, g, wr, w1, w3, w2, gf, final_norm, chunk=MOE_CHUNK, tg=MOE_TILE):
    n, d = x.shape
    n_exp, _, f = w1.shape
    nc = n // chunk
    nt = (TOP_K * n + n_exp * (tg - 1)) // tg
    npairs = nt + n_exp * nc

    hn, meta, metat, cnt = _route(x, g, wr, chunk)

    after = cnt[:, 0, :n_exp].astype(jnp.int32)
    cb = jnp.concatenate([jnp.zeros((1, n_exp), jnp.int32), after], axis=0)
    tiles_e = (cb[-1] + tg - 1) // tg
    tend = jnp.cumsum(tiles_e)
    nvalid = tend[-1]
    r = jnp.arange(nt, dtype=jnp.int32)
    rc = jnp.minimum(r, nvalid - 1)
    texp = jnp.sum((rc[:, None] >= tend[None, :]).astype(jnp.int32), axis=1)
    trank0 = (rc - (tend - tiles_e)[texp]) * tg
    lo = cb[:-1][:, texp].T
    hi = cb[1:][:, texp].T
    hit = ((r < nvalid)[:, None] & (hi > lo)
           & (lo < trank0[:, None] + tg) & (hi > trank0[:, None]))
    g_tile, g_chunk, g_first, _, g_valid = _pair_list(hit, npairs)
    c_chunk, c_tile, c_first, c_last, c_valid = _pair_list(hit.T, npairs)
    nvalid1 = nvalid.reshape(1).astype(jnp.int32)

    xs = pl.pallas_call(
        _gather_body,
        grid_spec=pltpu.PrefetchScalarGridSpec(
            num_scalar_prefetch=6, grid=(npairs,),
            in_specs=[pl.BlockSpec((chunk, d), lambda i, pt, pc, *_: (pc[i], 0)),
                      pl.BlockSpec((M_FIELDS, chunk), lambda i, pt, pc, *_: (0, pc[i]))],
            out_specs=pl.BlockSpec((tg, d), lambda i, pt, *_: (pt[i], 0)),
            scratch_shapes=[pltpu.VMEM((tg, d), F32)]),
        out_shape=jax.ShapeDtypeStruct((nt * tg, d), BF16),
        compiler_params=_params(("arbitrary",)),
        name="moe_gather",
    )(g_tile, g_chunk, g_first, g_valid, texp, trank0, hn, metat)

    tile = lambda i, te, nv: (jnp.minimum(i, nv[0] - 1), 0)
    wexp = lambda i, te, nv: (te[i], 0, 0)
    y = pl.pallas_call(
        functools.partial(_expert_body, fc=512),
        grid_spec=pltpu.PrefetchScalarGridSpec(
            num_scalar_prefetch=2, grid=(nt,),
            in_specs=[pl.BlockSpec((tg, d), tile), pl.BlockSpec((None, d, f), wexp),
                      pl.BlockSpec((None, d, f), wexp), pl.BlockSpec((None, f, d), wexp)],
            out_specs=pl.BlockSpec((tg, d), tile),
            scratch_shapes=[pltpu.VMEM((tg, f), BF16)]),
        out_shape=jax.ShapeDtypeStruct((nt * tg, d), BF16),
        compiler_params=_params(("arbitrary",)),
        name="moe_expert",
    )(texp, nvalid1, xs, w1, w3, w2)

    return pl.pallas_call(
        functools.partial(_combine_body, final_norm=final_norm),
        grid_spec=pltpu.PrefetchScalarGridSpec(
            num_scalar_prefetch=7, grid=(npairs,),
            in_specs=[pl.BlockSpec((chunk, d), lambda i, pt, pc, *_: (pc[i], 0)),
                      pl.BlockSpec((chunk, LANES), lambda i, pt, pc, *_: (pc[i], 0)),
                      pl.BlockSpec((tg, d), lambda i, pt, *_: (pt[i], 0)),
                      pl.BlockSpec((1, d), lambda i, *_: (0, 0))],
            out_specs=pl.BlockSpec((chunk, d), lambda i, pt, pc, *_: (pc[i], 0)),
            scratch_shapes=[pltpu.VMEM((chunk, d), F32)]),
        out_shape=jax.ShapeDtypeStruct((n, d), F32),
        compiler_params=_params(("arbitrary",)),
        name="moe_combine",
    )(c_tile, c_chunk, c_first, c_last, c_valid, texp, trank0, x, meta, y, gf.reshape(1, d))


def _norm_body(x_ref, g_ref, o_ref):
    o_ref[...] = _rms(x_ref[...], g_ref[...])


def _final_norm(x, g, tm):
    n, d = x.shape
    row = lambda i: (i, 0)
    return pl.pallas_call(
        _norm_body,
        grid=(n // tm,),
        in_specs=[pl.BlockSpec((tm, d), row), _resident((1, d))],
        out_specs=pl.BlockSpec((tm, d), row),
        out_shape=jax.ShapeDtypeStruct((n, d), F32),
        compiler_params=_params(("parallel",)),
        name="final_norm",
    )(x, g.reshape(1, d))


def _pad_front(state, rows):
    b, r, c = state.shape
    return jnp.concatenate([jnp.zeros((b, rows - r, c), state.dtype), state], axis=1)


def kernel(x_prompt, x_sample, cache_k, cache_v, state_conv, state_pool, page_table, norm_mix, w_in, conv_w, conv_b, conv_ln_g, conv_ln_b, w_conv_out, w_attn_out, sb_bias, pool_w, pool_scale, w_o, norm_ffn, ffn_w1, ffn_w3, ffn_w2, moe_router, moe_w1, moe_w3, moe_w2, norm_final):
    bp, tp, d = x_prompt.shape
    bs, tsm, _ = x_sample.shape
    depth = w_in.shape[0]
    a = N_HEADS * HEAD_DIM
    n_pages = page_table.shape[1]
    past = n_pages * PAGE_SIZE
    cshape = (cache_k.shape[0], cache_k.shape[1], a, PAGE_SIZE)
    ck = jnp.transpose(cache_k, (0, 1, 3, 4, 2)).reshape(cshape)
    cv = jnp.transpose(cache_v, (0, 1, 3, 4, 2)).reshape(cshape)

    xp = x_prompt.reshape(bp * tp, d)
    xs = x_sample.reshape(bs * tsm, d)
    outs = {k: [] for k in ("kp", "vp", "cp", "pp", "ks", "vs", "cs", "ps")}
    for l in range(depth):
        w_in_l = w_in[l].astype(BF16)
        lw = (conv_w[l], conv_b[l], conv_ln_g[l], conv_ln_b[l], w_conv_out[l].astype(BF16),
              w_attn_out[l].astype(BF16), pool_w[l].astype(BF16), pool_scale[l], w_o[l].astype(BF16))

        u, q, k, v, p, gate = _in_proj(xp, norm_mix[l], w_in_l, tm=512)
        c = u.shape[1]
        attn = _attn_prompt(q.reshape(bp, tp, a), k.reshape(bp, tp, a), v.reshape(bp, tp, a),
                            sb_bias[l])
        u3 = u.reshape(bp, tp, c)
        p3 = p.reshape(bp, tp, c)
        xp = _mix(xp, u3, p3, attn.reshape(bp * tp, a), gate,
                  jnp.zeros((bp, CONV_HIST, c), F32), jnp.zeros((bp, POOL_HIST, c), F32),
                  lw, nb=1, ts=256, pos0=0)
        outs["kp"].append(k.reshape(bp, tp, N_HEADS, HEAD_DIM))
        outs["vp"].append(v.reshape(bp, tp, N_HEADS, HEAD_DIM))
        outs["cp"].append(u3[:, tp - (CONV_W - 1):])
        outs["pp"].append(p3[:, tp - (POOL_MAX - 1):])

        u, q, k, v, p, gate = _in_proj(xs, norm_mix[l], w_in_l, tm=bs * tsm)
        attn = _attn_sample(q.reshape(bs, tsm, a), k.reshape(bs, tsm, a), v.reshape(bs, tsm, a),
                            ck, cv, l, page_table, sb_bias[l])
        u3 = u.reshape(bs, tsm, c)
        p3 = p.reshape(bs, tsm, c)
        xs = _mix(xs, u3, p3, attn.reshape(bs * tsm, a), gate,
                  _pad_front(state_conv[l], CONV_HIST), _pad_front(state_pool[l], POOL_HIST),
                  lw, nb=bs, ts=tsm, pos0=past)
        outs["ks"].append(k.reshape(bs, tsm, N_HEADS, HEAD_DIM))
        outs["vs"].append(v.reshape(bs, tsm, N_HEADS, HEAD_DIM))
        outs["cs"].append(jnp.concatenate([state_conv[l], u3], axis=1)[:, -(CONV_W - 1):])
        outs["ps"].append(jnp.concatenate([state_pool[l], p3], axis=1)[:, -(POOL_MAX - 1):])

        i = l // 2
        last = l == depth - 1
        if l % 2 == 0:
            w1, w3, w2 = ffn_w1[i].astype(BF16), ffn_w3[i].astype(BF16), ffn_w2[i].astype(BF16)
            xp = _ffn(xp, norm_ffn[l], w1, w3, w2, tm=512)
            xs = _ffn(xs, norm_ffn[l], w1, w3, w2, tm=bs * tsm)
            if last:
                xp = _final_norm(xp, norm_final, tm=512)
                xs = _final_norm(xs, norm_final, tm=bs * tsm)
        else:
            w1, w3, w2 = moe_w1[i].astype(BF16), moe_w3[i].astype(BF16), moe_w2[i].astype(BF16)
            x_all = _moe(jnp.concatenate([xp, xs], axis=0), norm_ffn[l], moe_router[i],
                         w1, w3, w2, norm_final, last)
            xp, xs = x_all[:bp * tp], x_all[bp * tp:]

    return (xp.reshape(bp, tp, d), xs.reshape(bs, tsm, d),
            jnp.stack(outs["kp"]), jnp.stack(outs["vp"]), jnp.stack(outs["cp"]), jnp.stack(outs["pp"]),
            jnp.stack(outs["ks"]), jnp.stack(outs["vs"]), jnp.stack(outs["cs"]), jnp.stack(outs["ps"]))
```

```python
import functools

import jax
import jax.numpy as jnp
from jax import lax
from jax.experimental import pallas as pl
from jax.experimental.pallas import tpu as pltpu

F32 = jnp.float32
BF16 = jnp.bfloat16
EPS = 1e-6
LOG2E = 1.4426950408889634

CONV_W = 31
POOL_WINDOWS = (2, 4, 8, 16)
POOL_MAX = 16
N_HEADS = 8
HEAD_DIM = 64
PAGE_SIZE = 128
TOP_K = 2
CONV_HIST = 32
POOL_HIST = 16
LANES = 128
VMEM_LIMIT = 56 * 1024 * 1024


def _params(sem):
    return pltpu.CompilerParams(dimension_semantics=sem, vmem_limit_bytes=VMEM_LIMIT)


def _resident(shape):
    zeros = (0,) * len(shape)
    return pl.BlockSpec(shape, lambda *_: zeros, pipeline_mode=pl.Buffered(1))


def _rms(x, g):
    return x * lax.rsqrt(jnp.mean(x * x, axis=-1, keepdims=True) + EPS) * g


def _softplus(z):
    return jnp.maximum(z, 0.0) + jnp.log(1.0 + jnp.exp2(jnp.abs(z) * -LOG2E))


def _split_bf16(x):
    hi = x.astype(BF16)
    return hi, (x - hi.astype(F32)).astype(BF16)


def _in_proj_body(x_ref, g_ref, w_ref, u_ref, q_ref, k_ref, v_ref, p_ref, gate_ref, *, q_scale):
    hn = _rms(x_ref[...], g_ref[...]).astype(BF16)
    cd = u_ref.shape[1]
    ad = q_ref.shape[1]
    pd = p_ref.shape[1]

    def proj(c0, n):
        return jnp.dot(hn, w_ref[:, c0:c0 + n], preferred_element_type=F32)

    c = 0
    a = proj(c, cd)
    b = proj(c + cd, cd)
    u_ref[...] = a * jax.nn.sigmoid(b)
    c += 2 * cd
    q_ref[...] = (proj(c, ad) * q_scale).astype(BF16)
    c += ad
    k_ref[...] = proj(c, ad)
    c += ad
    v_ref[...] = proj(c, ad)
    c += ad
    p_ref[...] = proj(c, pd)
    c += pd
    ng = gate_ref.shape[1]
    step = 512
    for j in range(0, ng, step):
        gate_ref[:, j:j + step] = jax.nn.sigmoid(proj(c + j, step)).astype(BF16)


def _in_proj(x, g, w, tm):
    n, d = x.shape
    assert n % tm == 0
    cd = ad = pd = 512
    ng = w.shape[1] - 2 * cd - 3 * ad - pd
    row = lambda i: (i, 0)
    outs = [
        jax.ShapeDtypeStruct((n, cd), F32),
        jax.ShapeDtypeStruct((n, ad), BF16),
        jax.ShapeDtypeStruct((n, ad), F32),
        jax.ShapeDtypeStruct((n, ad), F32),
        jax.ShapeDtypeStruct((n, pd), F32),
        jax.ShapeDtypeStruct((n, ng), BF16),
    ]
    return pl.pallas_call(
        functools.partial(_in_proj_body, q_scale=HEAD_DIM ** -0.5),
        grid=(n // tm,),
        in_specs=[pl.BlockSpec((tm, d), row), _resident((1, d)), _resident(w.shape)],
        out_specs=[pl.BlockSpec((tm, o.shape[1]), row) for o in outs],
        out_shape=outs,
        compiler_params=_params(("parallel",)),
        name="in_proj",
    )(x, g.reshape(1, d), w)


def _attn_prompt_body(bias_ref, q_ref, k_ref, v_ref, o_ref, vt_scr, acc_scr, *, tq, hp):
    t_len = q_ref.shape[0]
    nb = t_len // tq
    hd = HEAD_DIM
    g = pl.program_id(1)
    for j in range(nb):
        vt_scr[j] = v_ref[j * tq:(j + 1) * tq, :].T.astype(BF16)
    key = lax.broadcasted_iota(jnp.int32, (tq, tq), 0)
    qry = lax.broadcasted_iota(jnp.int32, (tq, tq), 1)
    valid = key < qry
    tri = (qry >= key).astype(BF16)

    biases = [bias_ref[g * hp + hh] for hh in range(hp)]

    def block(qi, kj, cs_in, diag):
        q_rows = pl.ds(pl.multiple_of(qi * tq, tq), tq)
        k_rows = pl.ds(pl.multiple_of(kj * tq, tq), tq)
        heads = [slice(hh * hd, (hh + 1) * hd) for hh in range(hp)]
        zs = [lax.dot_general(k_ref[k_rows, hs].astype(BF16), q_ref[q_rows, hs],
                              (((1,), (1,)), ((), ())), preferred_element_type=F32) + biases[hh]
              for hh, hs in enumerate(heads)]
        sps = [_softplus(z) for z in zs]
        if diag:
            sps = [jnp.where(valid, sp, 0.0) for sp in sps]
        css = [jnp.dot(tri, sp.astype(BF16), preferred_element_type=F32) for sp in sps]
        ws = [jnp.exp(z - cs - c) for z, cs, c in zip(zs, css, cs_in)]
        if diag:
            ws = [jnp.where(valid, w, 0.0) for w in ws]
        pvs = [jnp.dot(vt_scr[kj, hs, :], w.astype(BF16), preferred_element_type=F32)
               for hs, w in zip(heads, ws)]
        for hs, pv in zip(heads, pvs):
            if diag:
                acc_scr[qi, hs, :] = pv
            else:
                acc_scr[qi, hs, :] += pv
        return tuple(c + cs[0:1, :] for c, cs in zip(cs_in, css))

    def q_loop(qi, carry):
        c0 = block(qi, qi, (jnp.zeros((1, tq), F32),) * hp, True)
        lax.fori_loop(0, qi, lambda i, c: block(qi, qi - 1 - i, c, False), c0)
        return carry

    lax.fori_loop(0, nb, q_loop, 0)
    for j in range(nb):
        o_ref[j * tq:(j + 1) * tq, :] = acc_scr[j].T.astype(BF16)


def _attn_prompt(q, k, v, bias, tq=256, hp=8):
    b, t, a = q.shape
    w = hp * HEAD_DIM
    assert t % tq == 0 and a % w == 0
    blk = lambda bi, gi: (bi, 0, gi)
    spec = pl.BlockSpec((None, t, w), blk)
    return pl.pallas_call(
        functools.partial(_attn_prompt_body, tq=tq, hp=hp),
        grid=(b, a // w),
        in_specs=[pl.BlockSpec(memory_space=pltpu.SMEM), spec, spec, spec],
        out_specs=spec,
        out_shape=jax.ShapeDtypeStruct((b, t, a), BF16),
        scratch_shapes=[pltpu.VMEM((t // tq, w, tq), BF16), pltpu.VMEM((t // tq, w, tq), F32)],
        compiler_params=_params(("parallel", "parallel")),
        name="attn_prompt",
    )(bias, q, k, v)


def _attn_sample_body(pt_ref, bias_ref, q_ref, kn_ref, vn_ref, *rest, pp):
    kpg = rest[:pp]
    vpg = rest[pp:2 * pp]
    o_ref = rest[2 * pp]
    qbd_scr, bcol_scr, c_scr, acc_scr, pad_scr = rest[2 * pp + 1:]
    s = pl.program_id(1)
    t_new, a = q_ref.shape
    hd = HEAD_DIM
    nh = a // hd
    nq = nh * t_new

    def chunk(kt, vt, masked):
        r = kt.shape[1]
        sb = min(r, 2 * PAGE_SIZE)
        nblk = r // sb
        summed = lax.broadcasted_iota(jnp.int32, (sb, sb), 0)
        at = lax.broadcasted_iota(jnp.int32, (sb, sb), 1)
        tri = (summed >= at).astype(BF16)
        z = jnp.dot(qbd_scr[...], kt, preferred_element_type=F32) + bcol_scr[...]
        sp = _softplus(z)
        if masked:
            key = lax.broadcasted_iota(jnp.int32, z.shape, 1)
            qry = lax.broadcasted_iota(jnp.int32, z.shape, 0) % t_new
            ok = key < qry
            sp = jnp.where(ok, sp, 0.0)
        sp_st = jnp.concatenate([sp[:, j * sb:(j + 1) * sb] for j in range(nblk)], axis=0)
        cs_st = jnp.dot(sp_st.astype(BF16), tri, preferred_element_type=F32)
        run = c_scr[...]
        ws = [None] * nblk
        for j in reversed(range(nblk)):
            cs = cs_st[j * nq:(j + 1) * nq, :]
            ws[j] = jnp.exp(z[:, j * sb:(j + 1) * sb] - cs - run)
            run = run + cs[:, 0:1]
        c_scr[...] = run
        w = jnp.concatenate(ws, axis=1)
        if masked:
            w = jnp.where(ok, w, 0.0)
        acc_scr[...] += lax.dot_general(w.astype(BF16), vt, (((1,), (1,)), ((), ())),
                                        preferred_element_type=F32)

    @pl.when(s == 0)
    def _():
        qt = jnp.concatenate([q_ref[...].astype(F32)] * nh, axis=0)
        hq = lax.broadcasted_iota(jnp.int32, qt.shape, 0)
        col = lax.broadcasted_iota(jnp.int32, qt.shape, 1)
        qbd_scr[...] = jnp.where(hq // t_new == col // hd, qt, 0.0).astype(BF16)
        row = lax.broadcasted_iota(jnp.int32, (nq, 1), 0)
        bcol = jnp.zeros((nq, 1), F32)
        for h in range(nh):
            bcol = jnp.where(row // t_new == h, bias_ref[h], bcol)
        bcol_scr[...] = bcol
        c_scr[...] = jnp.zeros_like(c_scr)
        acc_scr[...] = jnp.zeros_like(acc_scr)
        pad_scr[...] = jnp.zeros_like(pad_scr)
        pad_scr[0:t_new, :] = kn_ref[...]
        knt = pad_scr[...].T.astype(BF16)
        pad_scr[0:t_new, :] = vn_ref[...]
        vnt = pad_scr[...].T.astype(BF16)
        chunk(knt, vnt, True)

    kt = jnp.concatenate([kpg[i][...].astype(BF16) for i in reversed(range(pp))], axis=1)
    vt = jnp.concatenate([vpg[i][...].astype(BF16) for i in reversed(range(pp))], axis=1)
    chunk(kt, vt, False)

    @pl.when(s == pl.num_programs(1) - 1)
    def _():
        col = lax.broadcasted_iota(jnp.int32, (t_new, a), 1)
        out = jnp.zeros((t_new, a), F32)
        for h in range(nh):
            out = out + jnp.where(col // hd == h, acc_scr[h * t_new:(h + 1) * t_new, :], 0.0)
        o_ref[...] = out.astype(BF16)


def _attn_sample(q, kn, vn, cache_kt, cache_vt, layer, page_table, bias, pp=32):
    b, t, a = q.shape
    n_pages = page_table.shape[1]
    assert n_pages % pp == 0 and t <= PAGE_SIZE
    nq = (a // HEAD_DIM) * t
    new_spec = pl.BlockSpec((None, t, a), lambda bi, si, pt: (bi, 0, 0))

    def page_spec(i):
        return pl.BlockSpec(
            (None, None, a, PAGE_SIZE),
            lambda bi, si, pt: (layer, pt[bi, n_pages - 1 - (si * pp + i)], 0, 0))

    grid_spec = pltpu.PrefetchScalarGridSpec(
        num_scalar_prefetch=1,
        grid=(b, n_pages // pp),
        in_specs=([pl.BlockSpec(memory_space=pltpu.SMEM), new_spec, new_spec, new_spec]
                  + [page_spec(i) for i in range(pp)] * 2),
        out_specs=new_spec,
        scratch_shapes=[
            pltpu.VMEM((nq, a), BF16),
            pltpu.VMEM((nq, 1), F32),
            pltpu.VMEM((nq, 1), F32),
            pltpu.VMEM((nq, a), F32),
            pltpu.VMEM((PAGE_SIZE, a), F32),
        ],
    )
    return pl.pallas_call(
        functools.partial(_attn_sample_body, pp=pp),
        grid_spec=grid_spec,
        out_shape=jax.ShapeDtypeStruct((b, t, a), BF16),
        compiler_params=_params(("parallel", "arbitrary")),
        name="attn_sample",
    )(page_table, bias, q, kn, vn, *([cache_kt] * pp), *([cache_vt] * pp))


def _mix_body(*refs, has_hist, pos0):
    if has_hist:
        (x_ref, u_ref, p_ref, attn_ref, gate_ref, ust_ref, pst_ref, uh_ref, ph_ref,
         cw_ref, cb_ref, lg_ref, lb_ref, wco_ref, wao_ref, pw_ref, ps_ref, wo_ref,
         o_ref, uext, pext, ycv, ush) = refs
    else:
        (x_ref, u_ref, p_ref, attn_ref, gate_ref, ust_ref, pst_ref,
         cw_ref, cb_ref, lg_ref, lb_ref, wco_ref, wao_ref, pw_ref, ps_ref, wo_ref,
         o_ref, uext, pext, ycv, ush) = refs
    i = pl.program_id(1)
    nb, ts, c = u_ref.shape
    m = nb * ts
    d = x_ref.shape[1]

    if has_hist:
        first = i == 0
        uext[:, 0:CONV_HIST, :] = jnp.where(first, ust_ref[...], uh_ref[...])
        pext[:, 0:POOL_HIST, :] = jnp.where(first, pst_ref[...], ph_ref[...])
    else:
        uext[:, 0:CONV_HIST, :] = ust_ref[...]
        pext[:, 0:POOL_HIST, :] = pst_ref[...]
    uext[:, CONV_HIST:CONV_HIST + ts, :] = u_ref[...]
    pext[:, POOL_HIST:POOL_HIST + ts, :] = p_ref[...]

    off = CONV_HIST - (CONV_W - 1)
    for b in range(1, 8):
        ush[b - 1] = uext[:, b:b + ush.shape[2], :]
    rc = min(ts, 64)
    for lg in range(c // LANES):
        ls = slice(lg * LANES, (lg + 1) * LANES)
        for r0 in range(0, ts, rc):
            acc = jnp.zeros((nb, rc, LANES), F32)
            for j in range(CONV_W):
                b = (off + j) % 8
                a = r0 + off + j - b
                win = uext[:, a:a + rc, ls] if b == 0 else ush[b - 1, :, a:a + rc, ls]
                acc = acc + cw_ref[j:j + 1, ls] * win
            ycv[:, r0:r0 + rc, ls] = acc

    yf = ycv[...].reshape(m, c) + cb_ref[...]
    mu = jnp.mean(yf, axis=-1, keepdims=True)
    yc = yf - mu
    var = jnp.mean(yc * yc, axis=-1, keepdims=True)
    yn = yc * lax.rsqrt(var + EPS) * lg_ref[...] + lb_ref[...]
    y_conv = jnp.dot((yn * jax.nn.sigmoid(yn)).astype(BF16), wco_ref[...],
                     preferred_element_type=F32)

    y_attn = jnp.dot(attn_ref[...], wao_ref[...], preferred_element_type=F32)

    pos = pos0 + i * ts + lax.broadcasted_iota(jnp.int32, (1, ts, 1), 1)
    pooled = []
    for gi, win in enumerate(POOL_WINDOWS):
        ls = slice(gi * LANES, (gi + 1) * LANES)
        cur = pext[:, POOL_HIST:POOL_HIST + ts, ls]
        ws = cur
        for k in range(1, win):
            ws = ws + pext[:, POOL_HIST - k:POOL_HIST - k + ts, ls]
        cnt = jnp.minimum(pos + 1, win).astype(F32)
        pg = (ws / cnt - cur).reshape(m, LANES).astype(BF16)
        pooled.append(jnp.dot(pg, pw_ref[gi], preferred_element_type=F32))
    y_pool = jnp.concatenate(pooled, axis=-1) * ps_ref[...]

    g0 = gate_ref[:, 0:d].astype(F32)
    g1 = gate_ref[:, d:2 * d].astype(F32)
    g2 = gate_ref[:, 2 * d:3 * d].astype(F32)
    merged = g0 * y_conv + g1 * y_attn + g2 * y_pool
    o_ref[...] = x_ref[...] + jnp.dot(merged.astype(BF16), wo_ref[...], preferred_element_type=F32)


def _mix(x, u, p, attn, gate, ust, pst, lw, nb, ts, pos0):
    (conv_w, conv_b, ln_g, ln_b, wco, wao, pool_w, pool_scale, wo) = lw
    b, t, c = u.shape
    d = x.shape[1]
    nt = t // ts
    has_hist = nt > 1
    assert t % ts == 0 and b % nb == 0 and (nb == 1 or not has_hist)
    tok = lambda bi, ti: (bi * nt + ti, 0)
    seq = lambda bi, ti: (bi, ti, 0)
    in_specs = [
        pl.BlockSpec((nb * ts, d), tok),
        pl.BlockSpec((nb, ts, c), seq),
        pl.BlockSpec((nb, ts, c), seq),
        pl.BlockSpec((nb * ts, c), tok),
        pl.BlockSpec((nb * ts, 3 * d), tok),
        pl.BlockSpec((nb, CONV_HIST, c), lambda bi, ti: (bi, 0, 0)),
        pl.BlockSpec((nb, POOL_HIST, c), lambda bi, ti: (bi, 0, 0)),
    ]
    args = [x, u, p, attn, gate, ust, pst]
    if has_hist:
        in_specs += [
            pl.BlockSpec((nb, CONV_HIST, c),
                         lambda bi, ti: (bi, jnp.maximum(ti * (ts // CONV_HIST) - 1, 0), 0)),
            pl.BlockSpec((nb, POOL_HIST, c),
                         lambda bi, ti: (bi, jnp.maximum(ti * (ts // POOL_HIST) - 1, 0), 0)),
        ]
        args += [u, p]
    consts = [conv_w, conv_b.reshape(1, c), ln_g.reshape(1, c), ln_b.reshape(1, c),
              wco, wao, pool_w, pool_scale.reshape(1, d), wo]
    in_specs += [_resident(a.shape) for a in consts]
    return pl.pallas_call(
        functools.partial(_mix_body, has_hist=has_hist, pos0=pos0),
        grid=(b // nb, nt),
        in_specs=in_specs,
        out_specs=pl.BlockSpec((nb * ts, d), tok),
        out_shape=jax.ShapeDtypeStruct(x.shape, F32),
        scratch_shapes=[pltpu.VMEM((nb, CONV_HIST + ts, c), F32),
                        pltpu.VMEM((nb, POOL_HIST + ts, c), F32),
                        pltpu.VMEM((nb, ts, c), F32),
                        pltpu.VMEM((7, nb, CONV_HIST + ts - 8, c), F32)],
        compiler_params=_params(("parallel", "parallel")),
        name="mix",
    )(*args, *consts)


def _swiglu_into(hn, w1_ref, w3_ref, a_scr, fc):
    f = a_scr.shape[1]
    for c0 in range(0, f, fc):
        n = min(fc, f - c0)
        h1 = jnp.dot(hn, w1_ref[:, c0:c0 + n], preferred_element_type=F32)
        h3 = jnp.dot(hn, w3_ref[:, c0:c0 + n], preferred_element_type=F32)
        a_scr[:, c0:c0 + n] = (h1 * jax.nn.sigmoid(h1) * h3).astype(BF16)


def _ffn_body(x_ref, g_ref, w1_ref, w3_ref, w2_ref, o_ref, a_scr, *, fc):
    x = x_ref[...]
    hn = _rms(x, g_ref[...]).astype(BF16)
    _swiglu_into(hn, w1_ref, w3_ref, a_scr, fc)
    o_ref[...] = x + jnp.dot(a_scr[...], w2_ref[...], preferred_element_type=F32)


def _ffn(x, g, w1, w3, w2, tm):
    n, d = x.shape
    assert n % tm == 0
    f = w1.shape[1]
    row = lambda i: (i, 0)
    return pl.pallas_call(
        functools.partial(_ffn_body, fc=512),
        grid=(n // tm,),
        in_specs=[pl.BlockSpec((tm, d), row), _resident((1, d)),
                  _resident(w1.shape), _resident(w3.shape), _resident(w2.shape)],
        out_specs=pl.BlockSpec((tm, d), row),
        out_shape=jax.ShapeDtypeStruct((n, d), F32),
        scratch_shapes=[pltpu.VMEM((tm, f), BF16)],
        compiler_params=_params(("parallel",)),
        name="ffn",
    )(x, g.reshape(1, d), w1, w3, w2)


MOE_CHUNK = 1280
MOE_TILE = 256
MOE_SUB = 256
M_E1, M_E2, M_POS1, M_POS2, M_P1, M_P2 = range(6)
M_FIELDS = 8


def _route_top2(hn_f32, wr_ref, n_exp):
    xh, xl = _split_bf16(hn_f32)
    wh, wl = _split_bf16(wr_ref[...])
    logits = (jnp.dot(xh, wh, preferred_element_type=F32)
              + jnp.dot(xh, wl, preferred_element_type=F32)
              + jnp.dot(xl, wh, preferred_element_type=F32))
    lane = lax.broadcasted_iota(jnp.int32, logits.shape, 1)
    neg = jnp.float32(-jnp.inf)
    l1 = jnp.where(lane < n_exp, logits, neg)
    m1 = jnp.max(l1, axis=-1, keepdims=True)
    i1 = jnp.min(jnp.where(l1 == m1, lane, LANES), axis=-1, keepdims=True)
    l2 = jnp.where(lane == i1, neg, l1)
    m2 = jnp.max(l2, axis=-1, keepdims=True)
    i2 = jnp.min(jnp.where(l2 == m2, lane, LANES), axis=-1, keepdims=True)
    e2 = jnp.exp(m2 - m1)
    return i1, i2, 1.0 / (1.0 + e2), e2 / (1.0 + e2)


def _route_body(x_ref, g_ref, wr_ref, hn_ref, meta_ref, metat_ref, cnt_ref, run_scr, *, n_exp, sub):
    @pl.when(pl.program_id(0) == 0)
    def _():
        run_scr[...] = jnp.zeros_like(run_scr)

    hn = _rms(x_ref[...], g_ref[...])
    hn_ref[...] = hn.astype(BF16)
    i1, i2, p1, p2 = _route_top2(hn, wr_ref, n_exp)
    m = hn.shape[0]
    lane = lax.broadcasted_iota(jnp.int32, (m, LANES), 1)
    sel1 = lane == i1
    sel2 = lane == i2
    assign = (sel1 | sel2).astype(BF16)
    row = lax.broadcasted_iota(jnp.int32, (sub, sub), 0)
    col = lax.broadcasted_iota(jnp.int32, (sub, sub), 1)
    before = (col < row).astype(BF16)
    run = run_scr[0:1, :]
    ranks = []
    for k, r0 in enumerate(range(0, m, sub)):
        a = assign[r0:r0 + sub, :]
        ranks.append(jnp.dot(before, a, preferred_element_type=F32) + run)
        run = run + jnp.sum(a.astype(F32), axis=0, keepdims=True)
        cnt_ref[k] = jnp.broadcast_to(run, cnt_ref.shape[1:])
    rank = jnp.concatenate(ranks, axis=0)
    run_scr[...] = jnp.broadcast_to(run, run_scr.shape)
    pos1 = jnp.sum(jnp.where(sel1, rank, 0.0), axis=-1, keepdims=True)
    pos2 = jnp.sum(jnp.where(sel2, rank, 0.0), axis=-1, keepdims=True)
    fields = {M_E1: i1.astype(F32), M_E2: i2.astype(F32), M_POS1: pos1, M_POS2: pos2,
              M_P1: p1, M_P2: p2}
    meta = jnp.zeros((m, LANES), F32)
    for j, val in fields.items():
        meta = jnp.where(lane == j, val, meta)
    meta_ref[...] = meta
    metat_ref[...] = meta.T[0:M_FIELDS, :]


def _route(x, g, wr, chunk, sub):
    n, d = x.shape
    n_exp = wr.shape[1]
    wr_pad = jnp.zeros((d, LANES), F32).at[:, :n_exp].set(wr)
    assert n % chunk == 0 and chunk % sub == 0
    nc = n // chunk
    spc = chunk // sub
    row = lambda c: (c, 0)
    return pl.pallas_call(
        functools.partial(_route_body, n_exp=n_exp, sub=sub),
        grid=(nc,),
        in_specs=[pl.BlockSpec((chunk, d), row), _resident((1, d)), _resident((d, LANES))],
        out_specs=[pl.BlockSpec((chunk, d), row), pl.BlockSpec((chunk, LANES), row),
                   pl.BlockSpec((M_FIELDS, chunk), lambda c: (0, c)),
                   pl.BlockSpec((spc, 8, LANES), lambda c: (c, 0, 0))],
        out_shape=[jax.ShapeDtypeStruct((n, d), BF16),
                   jax.ShapeDtypeStruct((n, LANES), F32),
                   jax.ShapeDtypeStruct((M_FIELDS, n), F32),
                   jax.ShapeDtypeStruct((n // sub, 8, LANES), F32)],
        scratch_shapes=[pltpu.VMEM((8, LANES), F32)],
        compiler_params=_params(("arbitrary",)),
        name="moe_route",
    )(x, g.reshape(1, d), wr_pad)


def _gather_body(ptile, pchunk, pfirst, pvalid, texp, trank0, hn_ref, metat_ref, o_ref, acc_scr):
    i = pl.program_id(0)

    @pl.when(pvalid[i] == 1)
    def _():
        r = ptile[i]
        e = texp[r]
        mt = metat_ref[...]
        e1 = mt[M_E1:M_E1 + 1, :].astype(jnp.int32)
        e2 = mt[M_E2:M_E2 + 1, :].astype(jnp.int32)
        pos = jnp.where(e1 == e, mt[M_POS1:M_POS1 + 1, :],
                        jnp.where(e2 == e, mt[M_POS2:M_POS2 + 1, :], -1.0))
        slot = pos.astype(jnp.int32) - trank0[r]
        rows = lax.broadcasted_iota(jnp.int32, (o_ref.shape[0], slot.shape[1]), 0)
        onehot = (rows == slot).astype(BF16)
        res = jnp.dot(onehot, hn_ref[...], preferred_element_type=F32)

        @pl.when(pfirst[i] == 1)
        def _():
            acc_scr[...] = res

        @pl.when(pfirst[i] == 0)
        def _():
            acc_scr[...] += res

        o_ref[...] = acc_scr[...].astype(BF16)


def _expert_body(texp, nvalid, xs_ref, w1_ref, w3_ref, w2_ref, y_ref, a_scr, *, fc):
    @pl.when(pl.program_id(0) < nvalid[0])
    def _():
        _swiglu_into(xs_ref[...], w1_ref, w3_ref, a_scr, fc)
        y_ref[...] = jnp.dot(a_scr[...], w2_ref[...], preferred_element_type=F32).astype(BF16)


def _combine_body(ptile, pchunk, pfirst, plast, pvalid, pbits, texp, trank0,
                  x_ref, meta_ref, y_ref, gf_ref, o_ref, acc_scr, *, final_norm, sub):
    i = pl.program_id(0)

    @pl.when(pvalid[i] == 1)
    def _():
        r = ptile[i]
        e = texp[r]
        m = meta_ref[...]
        is1 = m[:, M_E1:M_E1 + 1].astype(jnp.int32) == e
        is2 = m[:, M_E2:M_E2 + 1].astype(jnp.int32) == e
        pos = jnp.where(is1, m[:, M_POS1:M_POS1 + 1],
                        jnp.where(is2, m[:, M_POS2:M_POS2 + 1], -1.0))
        gate = jnp.where(is1, m[:, M_P1:M_P1 + 1], jnp.where(is2, m[:, M_P2:M_P2 + 1], 0.0))
        slot = pos.astype(jnp.int32) - trank0[r]

        @pl.when(pfirst[i] == 1)
        def _():
            acc_scr[...] = jnp.zeros_like(acc_scr)

        cols = lax.broadcasted_iota(jnp.int32, (sub, y_ref.shape[0]), 1)
        for k, r0 in enumerate(range(0, m.shape[0], sub)):
            @pl.when(((pbits[i] >> k) & 1) == 1)
            def _(r0=r0):
                onehot = (cols == slot[r0:r0 + sub, :]).astype(BF16)
                acc_scr[r0:r0 + sub, :] += gate[r0:r0 + sub, :] * jnp.dot(
                    onehot, y_ref[...], preferred_element_type=F32)

        @pl.when(plast[i] == 1)
        def _():
            out = x_ref[...] + acc_scr[...]
            if final_norm:
                out = _rms(out, gf_ref[...])
            o_ref[...] = out


def _pair_list(hit, npairs):
    nb = hit.shape[1]
    idx = jnp.nonzero(hit.reshape(-1), size=npairs, fill_value=-1)[0].astype(jnp.int32)
    valid = idx >= 0
    count = jnp.sum(valid.astype(jnp.int32))
    idx = jnp.where(valid, idx, idx[jnp.maximum(count - 1, 0)])
    a = idx // nb
    b = idx % nb
    off = jnp.full((1,), -1, jnp.int32)
    first = valid & (a != jnp.concatenate([off, a[:-1]]))
    nxt_valid = jnp.concatenate([valid[1:], jnp.zeros((1,), bool)])
    last = valid & ((a != jnp.concatenate([a[1:], off])) | ~nxt_valid)
    i32 = lambda v: v.astype(jnp.int32)
    return idx, a, b, i32(first), i32(last), i32(valid)


def _moe(x, g, wr, w1, w3, w2, gf, final_norm, chunk=MOE_CHUNK, tg=MOE_TILE, sub=MOE_SUB):
    n, d = x.shape
    n_exp, _, f = w1.shape
    nc = n // chunk
    spc = chunk // sub
    nt = (TOP_K * n + n_exp * (tg - 1)) // tg
    npairs = nt + n_exp * nc

    hn, meta, metat, cnt = _route(x, g, wr, chunk, sub)

    after = cnt[:, 0, :n_exp].astype(jnp.int32)
    cb = jnp.concatenate([jnp.zeros((1, n_exp), jnp.int32), after], axis=0)
    tiles_e = (cb[-1] + tg - 1) // tg
    tend = jnp.cumsum(tiles_e)
    nvalid = tend[-1]
    r = jnp.arange(nt, dtype=jnp.int32)
    rc = jnp.minimum(r, nvalid - 1)
    texp = jnp.sum((rc[:, None] >= tend[None, :]).astype(jnp.int32), axis=1)
    trank0 = (rc - (tend - tiles_e)[texp]) * tg
    lo = cb[:-1][:, texp].T
    hi = cb[1:][:, texp].T
    sub_hit = ((r < nvalid)[:, None] & (hi > lo)
               & (lo < trank0[:, None] + tg) & (hi > trank0[:, None])).reshape(nt, nc, spc)
    hit = jnp.any(sub_hit, axis=2)
    bits = jnp.sum(sub_hit.astype(jnp.int32) << jnp.arange(spc, dtype=jnp.int32), axis=2)
    _, g_tile, g_chunk, g_first, _, g_valid = _pair_list(hit, npairs)
    c_idx, c_chunk, c_tile, c_first, c_last, c_valid = _pair_list(hit.T, npairs)
    c_bits = bits.T.reshape(-1)[c_idx] * c_valid
    nvalid1 = nvalid.reshape(1).astype(jnp.int32)

    xs = pl.pallas_call(
        _gather_body,
        grid_spec=pltpu.PrefetchScalarGridSpec(
            num_scalar_prefetch=6, grid=(npairs,),
            in_specs=[pl.BlockSpec((chunk, d), lambda i, pt, pc, *_: (pc[i], 0)),
                      pl.BlockSpec((M_FIELDS, chunk), lambda i, pt, pc, *_: (0, pc[i]))],
            out_specs=pl.BlockSpec((tg, d), lambda i, pt, *_: (pt[i], 0)),
            scratch_shapes=[pltpu.VMEM((tg, d), F32)]),
        out_shape=jax.ShapeDtypeStruct((nt * tg, d), BF16),
        compiler_params=_params(("arbitrary",)),
        name="moe_gather",
    )(g_tile, g_chunk, g_first, g_valid, texp, trank0, hn, metat)

    tile = lambda i, te, nv: (jnp.minimum(i, nv[0] - 1), 0)
    wexp = lambda i, te, nv: (te[i], 0, 0)
    y = pl.pallas_call(
        functools.partial(_expert_body, fc=512),
        grid_spec=pltpu.PrefetchScalarGridSpec(
            num_scalar_prefetch=2, grid=(nt,),
            in_specs=[pl.BlockSpec((tg, d), tile), pl.BlockSpec((None, d, f), wexp),
                      pl.BlockSpec((None, d, f), wexp), pl.BlockSpec((None, f, d), wexp)],
            out_specs=pl.BlockSpec((tg, d), tile),
            scratch_shapes=[pltpu.VMEM((tg, f), BF16)]),
        out_shape=jax.ShapeDtypeStruct((nt * tg, d), BF16),
        compiler_params=_params(("arbitrary",)),
        name="moe_expert",
    )(texp, nvalid1, xs, w1, w3, w2)

    return pl.pallas_call(
        functools.partial(_combine_body, final_norm=final_norm, sub=sub),
        grid_spec=pltpu.PrefetchScalarGridSpec(
            num_scalar_prefetch=8, grid=(npairs,),
            in_specs=[pl.BlockSpec((chunk, d), lambda i, pt, pc, *_: (pc[i], 0)),
                      pl.BlockSpec((chunk, LANES), lambda i, pt, pc, *_: (pc[i], 0)),
                      pl.BlockSpec((tg, d), lambda i, pt, *_: (pt[i], 0)),
                      pl.BlockSpec((1, d), lambda i, *_: (0, 0))],
            out_specs=pl.BlockSpec((chunk, d), lambda i, pt, pc, *_: (pc[i], 0)),
            scratch_shapes=[pltpu.VMEM((chunk, d), F32)]),
        out_shape=jax.ShapeDtypeStruct((n, d), F32),
        compiler_params=_params(("arbitrary",)),
        name="moe_combine",
    )(c_tile, c_chunk, c_first, c_last, c_valid, c_bits, texp, trank0,
      x, meta, y, gf.reshape(1, d))


def _norm_body(x_ref, g_ref, o_ref):
    o_ref[...] = _rms(x_ref[...], g_ref[...])


def _final_norm(x, g, tm):
    n, d = x.shape
    row = lambda i: (i, 0)
    return pl.pallas_call(
        _norm_body,
        grid=(n // tm,),
        in_specs=[pl.BlockSpec((tm, d), row), _resident((1, d))],
        out_specs=pl.BlockSpec((tm, d), row),
        out_shape=jax.ShapeDtypeStruct((n, d), F32),
        compiler_params=_params(("parallel",)),
        name="final_norm",
    )(x, g.reshape(1, d))


def _pad_front(state, rows):
    b, r, c = state.shape
    return jnp.concatenate([jnp.zeros((b, rows - r, c), state.dtype), state], axis=1)


def kernel(x_prompt, x_sample, cache_k, cache_v, state_conv, state_pool, page_table, norm_mix, w_in, conv_w, conv_b, conv_ln_g, conv_ln_b, w_conv_out, w_attn_out, sb_bias, pool_w, pool_scale, w_o, norm_ffn, ffn_w1, ffn_w3, ffn_w2, moe_router, moe_w1, moe_w3, moe_w2, norm_final):
    bp, tp, d = x_prompt.shape
    bs, tsm, _ = x_sample.shape
    depth = w_in.shape[0]
    a = N_HEADS * HEAD_DIM
    n_pages = page_table.shape[1]
    past = n_pages * PAGE_SIZE
    cshape = (cache_k.shape[0], cache_k.shape[1], a, PAGE_SIZE)
    ck = jnp.transpose(cache_k, (0, 1, 3, 4, 2)).reshape(cshape)
    cv = jnp.transpose(cache_v, (0, 1, 3, 4, 2)).reshape(cshape)

    xp = x_prompt.reshape(bp * tp, d)
    xs = x_sample.reshape(bs * tsm, d)
    outs = {k: [] for k in ("kp", "vp", "cp", "pp", "ks", "vs", "cs", "ps")}
    for l in range(depth):
        w_in_l = w_in[l].astype(BF16)
        lw = (conv_w[l], conv_b[l], conv_ln_g[l], conv_ln_b[l], w_conv_out[l].astype(BF16),
              w_attn_out[l].astype(BF16), pool_w[l].astype(BF16), pool_scale[l], w_o[l].astype(BF16))

        u, q, k, v, p, gate = _in_proj(xp, norm_mix[l], w_in_l, tm=512)
        c = u.shape[1]
        attn = _attn_prompt(q.reshape(bp, tp, a), k.reshape(bp, tp, a), v.reshape(bp, tp, a),
                            sb_bias[l])
        u3 = u.reshape(bp, tp, c)
        p3 = p.reshape(bp, tp, c)
        xp = _mix(xp, u3, p3, attn.reshape(bp * tp, a), gate,
                  jnp.zeros((bp, CONV_HIST, c), F32), jnp.zeros((bp, POOL_HIST, c), F32),
                  lw, nb=1, ts=256, pos0=0)
        outs["kp"].append(k.reshape(bp, tp, N_HEADS, HEAD_DIM))
        outs["vp"].append(v.reshape(bp, tp, N_HEADS, HEAD_DIM))
        outs["cp"].append(u3[:, tp - (CONV_W - 1):])
        outs["pp"].append(p3[:, tp - (POOL_MAX - 1):])

        u, q, k, v, p, gate = _in_proj(xs, norm_mix[l], w_in_l, tm=bs * tsm)
        attn = _attn_sample(q.reshape(bs, tsm, a), k.reshape(bs, tsm, a), v.reshape(bs, tsm, a),
                            ck, cv, l, page_table, sb_bias[l])
        u3 = u.reshape(bs, tsm, c)
        p3 = p.reshape(bs, tsm, c)
        xs = _mix(xs, u3, p3, attn.reshape(bs * tsm, a), gate,
                  _pad_front(state_conv[l], CONV_HIST), _pad_front(state_pool[l], POOL_HIST),
                  lw, nb=bs, ts=tsm, pos0=past)
        outs["ks"].append(k.reshape(bs, tsm, N_HEADS, HEAD_DIM))
        outs["vs"].append(v.reshape(bs, tsm, N_HEADS, HEAD_DIM))
        outs["cs"].append(jnp.concatenate([state_conv[l], u3], axis=1)[:, -(CONV_W - 1):])
        outs["ps"].append(jnp.concatenate([state_pool[l], p3], axis=1)[:, -(POOL_MAX - 1):])

        i = l // 2
        last = l == depth - 1
        if l % 2 == 0:
            w1, w3, w2 = ffn_w1[i].astype(BF16), ffn_w3[i].astype(BF16), ffn_w2[i].astype(BF16)
            xp = _ffn(xp, norm_ffn[l], w1, w3, w2, tm=512)
            xs = _ffn(xs, norm_ffn[l], w1, w3, w2, tm=bs * tsm)
            if last:
                xp = _final_norm(xp, norm_final, tm=512)
                xs = _final_norm(xs, norm_final, tm=bs * tsm)
        else:
            w1, w3, w2 = moe_w1[i].astype(BF16), moe_w3[i].astype(BF16), moe_w2[i].astype(BF16)
            x_all = _moe(jnp.concatenate([xp, xs], axis=0), norm_ffn[l], moe_router[i],
                         w1, w3, w2, norm_final, last)
            xp, xs = x_all[:bp * tp], x_all[bp * tp:]

    return (xp.reshape(bp, tp, d), xs.reshape(bs, tsm, d),
            jnp.stack(outs["kp"]), jnp.stack(outs["vp"]), jnp.stack(outs["cp"]), jnp.stack(outs["pp"]),
            jnp.stack(outs["ks"]), jnp.stack(outs["vs"]), jnp.stack(outs["cs"]), jnp.stack(outs["ps"]))
```

```python
import functools

import jax
import jax.numpy as jnp
from jax import lax
from jax.experimental import pallas as pl
from jax.experimental.pallas import tpu as pltpu

F32 = jnp.float32
BF16 = jnp.bfloat16
EPS = 1e-6
LOG2E = 1.4426950408889634

CONV_W = 31
POOL_WINDOWS = (2, 4, 8, 16)
POOL_MAX = 16
N_HEADS = 8
HEAD_DIM = 64
PAGE_SIZE = 128
TOP_K = 2
CONV_HIST = 32
POOL_HIST = 16
LANES = 128
VMEM_LIMIT = 56 * 1024 * 1024


def _params(sem):
    return pltpu.CompilerParams(dimension_semantics=sem, vmem_limit_bytes=VMEM_LIMIT)


def _resident(shape):
    zeros = (0,) * len(shape)
    return pl.BlockSpec(shape, lambda *_: zeros, pipeline_mode=pl.Buffered(1))


def _rms(x, g):
    return x * lax.rsqrt(jnp.mean(x * x, axis=-1, keepdims=True) + EPS) * g


def _softplus(z):
    return jnp.maximum(z, 0.0) + jnp.log(1.0 + jnp.exp2(jnp.abs(z) * -LOG2E))


def _split_bf16(x):
    hi = x.astype(BF16)
    return hi, (x - hi.astype(F32)).astype(BF16)


def _in_proj_body(x_ref, g_ref, w_ref, *rest, q_scale, kv_t):
    if kv_t:
        stage = rest[-1]
        rest = rest[:-1]
    u_ref, q_ref, k_ref, v_ref, p_ref, gate_ref = rest[-6:]

    def put_kv(ref, val):
        if kv_t:
            stage[...] = val
            ref[...] = stage[...].T
        else:
            ref[...] = val
    hn = _rms(x_ref[...], g_ref[...]).astype(BF16)
    cd = u_ref.shape[1]
    ad = q_ref.shape[1]
    pd = p_ref.shape[1]

    def proj(c0, n):
        return jnp.dot(hn, w_ref[:, c0:c0 + n], preferred_element_type=F32)

    c = 0
    a = proj(c, cd)
    b = proj(c + cd, cd)
    u_ref[...] = a * jax.nn.sigmoid(b)
    c += 2 * cd
    q_ref[...] = (proj(c, ad) * q_scale).astype(BF16)
    c += ad
    put_kv(k_ref, proj(c, ad))
    c += ad
    put_kv(v_ref, proj(c, ad))
    c += ad
    p_ref[...] = proj(c, pd)
    c += pd
    ng = gate_ref.shape[1]
    step = 512
    for j in range(0, ng, step):
        gate_ref[:, j:j + step] = jax.nn.sigmoid(proj(c + j, step)).astype(BF16)


def _in_proj(x, g, w, tm, kv_seq=None):
    n, d = x.shape
    assert n % tm == 0
    cd = ad = pd = 512
    ng = w.shape[1] - 2 * cd - 3 * ad - pd
    row = lambda i: (i, 0)
    kv_shape, kv_spec, prev = (n, ad), pl.BlockSpec((tm, ad), row), ()
    if kv_seq is not None:
        layer, depth, batch, seq, prev = kv_seq
        assert seq % tm == 0 and batch * seq == n
        tpb = seq // tm
        kv_shape = (depth, batch, ad, seq)
        kv_spec = pl.BlockSpec((None, None, ad, tm), lambda i: (layer, i // tpb, 0, i % tpb))
        prev = () if prev is None else tuple(prev)
    outs = [
        jax.ShapeDtypeStruct((n, cd), F32),
        jax.ShapeDtypeStruct((n, ad), BF16),
        jax.ShapeDtypeStruct(kv_shape, F32),
        jax.ShapeDtypeStruct(kv_shape, F32),
        jax.ShapeDtypeStruct((n, pd), F32),
        jax.ShapeDtypeStruct((n, ng), BF16),
    ]
    out_specs = [pl.BlockSpec((tm, o.shape[-1]), row) for o in outs]
    out_specs[2] = out_specs[3] = kv_spec
    return pl.pallas_call(
        functools.partial(_in_proj_body, q_scale=HEAD_DIM ** -0.5, kv_t=kv_seq is not None),
        grid=(n // tm,),
        in_specs=([pl.BlockSpec((tm, d), row), _resident((1, d)), _resident(w.shape)]
                  + [pl.BlockSpec(memory_space=pl.ANY)] * len(prev)),
        out_specs=out_specs,
        out_shape=outs,
        input_output_aliases={3 + j: 2 + j for j in range(len(prev))},
        scratch_shapes=[pltpu.VMEM((tm, ad), F32)] if kv_seq is not None else [],
        compiler_params=_params(("parallel",)),
        name="in_proj",
    )(x, g.reshape(1, d), w, *prev)


def _attn_prompt_body(bias_ref, q_ref, kt_ref, vt_ref, o_ref, k_scr, vt_scr, acc_scr, *, tq, hp):
    t_len = q_ref.shape[0]
    nb = t_len // tq
    hd = HEAD_DIM
    g = pl.program_id(1)
    for j in range(nb):
        k_scr[j] = kt_ref[:, j * tq:(j + 1) * tq].T.astype(BF16)
        vt_scr[j] = vt_ref[:, j * tq:(j + 1) * tq].astype(BF16)
    key = lax.broadcasted_iota(jnp.int32, (tq, tq), 0)
    qry = lax.broadcasted_iota(jnp.int32, (tq, tq), 1)
    valid = key < qry
    tri = (qry >= key).astype(BF16)

    biases = [bias_ref[g * hp + hh] for hh in range(hp)]

    def block(qi, kj, cs_in, diag):
        q_rows = pl.ds(pl.multiple_of(qi * tq, tq), tq)
        heads = [slice(hh * hd, (hh + 1) * hd) for hh in range(hp)]
        zs = [lax.dot_general(k_scr[kj, :, hs], q_ref[q_rows, hs],
                              (((1,), (1,)), ((), ())), preferred_element_type=F32) + biases[hh]
              for hh, hs in enumerate(heads)]
        sps = [_softplus(z) for z in zs]
        if diag:
            sps = [jnp.where(valid, sp, 0.0) for sp in sps]
        css = [jnp.dot(tri, sp.astype(BF16), preferred_element_type=F32) for sp in sps]
        ws = [jnp.exp(z - cs - c) for z, cs, c in zip(zs, css, cs_in)]
        if diag:
            ws = [jnp.where(valid, w, 0.0) for w in ws]
        pvs = [jnp.dot(vt_scr[kj, hs, :], w.astype(BF16), preferred_element_type=F32)
               for hs, w in zip(heads, ws)]
        for hs, pv in zip(heads, pvs):
            if diag:
                acc_scr[qi, hs, :] = pv
            else:
                acc_scr[qi, hs, :] += pv
        return tuple(c + cs[0:1, :] for c, cs in zip(cs_in, css))

    def q_loop(qi, carry):
        c0 = block(qi, qi, (jnp.zeros((1, tq), F32),) * hp, True)
        lax.fori_loop(0, qi, lambda i, c: block(qi, qi - 1 - i, c, False), c0)
        return carry

    lax.fori_loop(0, nb, q_loop, 0)
    for j in range(nb):
        o_ref[j * tq:(j + 1) * tq, :] = acc_scr[j].T.astype(BF16)


def _attn_prompt(q, kt, vt, layer, bias, tq=256, hp=8):
    b, t, a = q.shape
    w = hp * HEAD_DIM
    assert t % tq == 0 and a % w == 0
    spec = pl.BlockSpec((None, t, w), lambda bi, gi: (bi, 0, gi))
    spec_t = pl.BlockSpec((None, None, w, t), lambda bi, gi: (layer, bi, gi, 0))
    nb = t // tq
    return pl.pallas_call(
        functools.partial(_attn_prompt_body, tq=tq, hp=hp),
        grid=(b, a // w),
        in_specs=[pl.BlockSpec(memory_space=pltpu.SMEM), spec, spec_t, spec_t],
        out_specs=spec,
        out_shape=jax.ShapeDtypeStruct((b, t, a), BF16),
        scratch_shapes=[pltpu.VMEM((nb, tq, w), BF16), pltpu.VMEM((nb, w, tq), BF16),
                        pltpu.VMEM((nb, w, tq), F32)],
        compiler_params=_params(("parallel", "parallel")),
        name="attn_prompt",
    )(bias, q, kt, vt)


def _attn_sample_body(pt_ref, bias_ref, q_ref, kn_ref, vn_ref, *rest, pp):
    kpg = rest[:pp]
    vpg = rest[pp:2 * pp]
    o_ref = rest[2 * pp]
    qbd_scr, bcol_scr, c_scr, acc_scr, pad_scr = rest[2 * pp + 1:]
    s = pl.program_id(1)
    t_new, a = q_ref.shape
    hd = HEAD_DIM
    nh = a // hd
    nq = nh * t_new

    def chunk(kt, vt, masked):
        r = kt.shape[1]
        sb = min(r, 2 * PAGE_SIZE)
        nblk = r // sb
        summed = lax.broadcasted_iota(jnp.int32, (sb, sb), 0)
        at = lax.broadcasted_iota(jnp.int32, (sb, sb), 1)
        tri = (summed >= at).astype(BF16)
        z = jnp.dot(qbd_scr[...], kt, preferred_element_type=F32) + bcol_scr[...]
        sp = _softplus(z)
        if masked:
            key = lax.broadcasted_iota(jnp.int32, z.shape, 1)
            qry = lax.broadcasted_iota(jnp.int32, z.shape, 0) % t_new
            ok = key < qry
            sp = jnp.where(ok, sp, 0.0)
        sp_st = jnp.concatenate([sp[:, j * sb:(j + 1) * sb] for j in range(nblk)], axis=0)
        cs_st = jnp.dot(sp_st.astype(BF16), tri, preferred_element_type=F32)
        run = c_scr[...]
        ws = [None] * nblk
        for j in reversed(range(nblk)):
            cs = cs_st[j * nq:(j + 1) * nq, :]
            ws[j] = jnp.exp(z[:, j * sb:(j + 1) * sb] - cs - run)
            run = run + cs[:, 0:1]
        c_scr[...] = run
        w = jnp.concatenate(ws, axis=1)
        if masked:
            w = jnp.where(ok, w, 0.0)
        acc_scr[...] += lax.dot_general(w.astype(BF16), vt, (((1,), (1,)), ((), ())),
                                        preferred_element_type=F32)

    @pl.when(s == 0)
    def _():
        qt = jnp.concatenate([q_ref[...].astype(F32)] * nh, axis=0)
        hq = lax.broadcasted_iota(jnp.int32, qt.shape, 0)
        col = lax.broadcasted_iota(jnp.int32, qt.shape, 1)
        qbd_scr[...] = jnp.where(hq // t_new == col // hd, qt, 0.0).astype(BF16)
        row = lax.broadcasted_iota(jnp.int32, (nq, 1), 0)
        bcol = jnp.zeros((nq, 1), F32)
        for h in range(nh):
            bcol = jnp.where(row // t_new == h, bias_ref[h], bcol)
        bcol_scr[...] = bcol
        c_scr[...] = jnp.zeros_like(c_scr)
        acc_scr[...] = jnp.zeros_like(acc_scr)
        pad_scr[...] = jnp.zeros_like(pad_scr)
        pad_scr[0:t_new, :] = kn_ref[...]
        knt = pad_scr[...].T.astype(BF16)
        pad_scr[0:t_new, :] = vn_ref[...]
        vnt = pad_scr[...].T.astype(BF16)
        chunk(knt, vnt, True)

    kt = jnp.concatenate([kpg[i][...].astype(BF16) for i in reversed(range(pp))], axis=1)
    vt = jnp.concatenate([vpg[i][...].astype(BF16) for i in reversed(range(pp))], axis=1)
    chunk(kt, vt, False)

    @pl.when(s == pl.num_programs(1) - 1)
    def _():
        col = lax.broadcasted_iota(jnp.int32, (t_new, a), 1)
        out = jnp.zeros((t_new, a), F32)
        for h in range(nh):
            out = out + jnp.where(col // hd == h, acc_scr[h * t_new:(h + 1) * t_new, :], 0.0)
        o_ref[...] = out.astype(BF16)


def _attn_sample(q, kn, vn, cache_kt, cache_vt, layer, page_table, bias, pp=32):
    b, t, a = q.shape
    n_pages = page_table.shape[1]
    assert n_pages % pp == 0 and t <= PAGE_SIZE
    nq = (a // HEAD_DIM) * t
    new_spec = pl.BlockSpec((None, t, a), lambda bi, si, pt: (bi, 0, 0))

    def page_spec(i):
        return pl.BlockSpec(
            (None, None, a, PAGE_SIZE),
            lambda bi, si, pt: (layer, pt[bi, n_pages - 1 - (si * pp + i)], 0, 0))

    grid_spec = pltpu.PrefetchScalarGridSpec(
        num_scalar_prefetch=1,
        grid=(b, n_pages // pp),
        in_specs=([pl.BlockSpec(memory_space=pltpu.SMEM), new_spec, new_spec, new_spec]
                  + [page_spec(i) for i in range(pp)] * 2),
        out_specs=new_spec,
        scratch_shapes=[
            pltpu.VMEM((nq, a), BF16),
            pltpu.VMEM((nq, 1), F32),
            pltpu.VMEM((nq, 1), F32),
            pltpu.VMEM((nq, a), F32),
            pltpu.VMEM((PAGE_SIZE, a), F32),
        ],
    )
    return pl.pallas_call(
        functools.partial(_attn_sample_body, pp=pp),
        grid_spec=grid_spec,
        out_shape=jax.ShapeDtypeStruct((b, t, a), BF16),
        compiler_params=_params(("parallel", "arbitrary")),
        name="attn_sample",
    )(page_table, bias, q, kn, vn, *([cache_kt] * pp), *([cache_vt] * pp))


def _mix_body(*refs, has_hist, has_prev, pos0):
    if has_prev:
        n_in = 16 + (2 if has_hist else 0)
        refs = refs[:n_in] + refs[n_in + 1:]
    if has_hist:
        (x_ref, u_ref, p_ref, attn_ref, gate_ref, ust_ref, pst_ref, uh_ref, ph_ref,
         cw_ref, cb_ref, lg_ref, lb_ref, wco_ref, wao_ref, pw_ref, ps_ref, wo_ref,
         o_ref, uext, pext, ycv, ush) = refs
    else:
        (x_ref, u_ref, p_ref, attn_ref, gate_ref, ust_ref, pst_ref,
         cw_ref, cb_ref, lg_ref, lb_ref, wco_ref, wao_ref, pw_ref, ps_ref, wo_ref,
         o_ref, uext, pext, ycv, ush) = refs
    i = pl.program_id(1)
    nb, ts, c = u_ref.shape
    m = nb * ts
    d = x_ref.shape[1]

    if has_hist:
        first = i == 0
        uext[:, 0:CONV_HIST, :] = jnp.where(first, ust_ref[...], uh_ref[...])
        pext[:, 0:POOL_HIST, :] = jnp.where(first, pst_ref[...], ph_ref[...])
    else:
        uext[:, 0:CONV_HIST, :] = ust_ref[...]
        pext[:, 0:POOL_HIST, :] = pst_ref[...]
    uext[:, CONV_HIST:CONV_HIST + ts, :] = u_ref[...]
    pext[:, POOL_HIST:POOL_HIST + ts, :] = p_ref[...]

    off = CONV_HIST - (CONV_W - 1)
    for b in range(1, 8):
        ush[b - 1] = uext[:, b:b + ush.shape[2], :]
    rc = min(ts, 64)
    for lg in range(c // LANES):
        ls = slice(lg * LANES, (lg + 1) * LANES)
        for r0 in range(0, ts, rc):
            acc = jnp.zeros((nb, rc, LANES), F32)
            for j in range(CONV_W):
                b = (off + j) % 8
                a = r0 + off + j - b
                win = uext[:, a:a + rc, ls] if b == 0 else ush[b - 1, :, a:a + rc, ls]
                acc = acc + cw_ref[j:j + 1, ls] * win
            ycv[:, r0:r0 + rc, ls] = acc

    yf = ycv[...].reshape(m, c) + cb_ref[...]
    mu = jnp.mean(yf, axis=-1, keepdims=True)
    yc = yf - mu
    var = jnp.mean(yc * yc, axis=-1, keepdims=True)
    yn = yc * lax.rsqrt(var + EPS) * lg_ref[...] + lb_ref[...]
    y_conv = jnp.dot((yn * jax.nn.sigmoid(yn)).astype(BF16), wco_ref[...],
                     preferred_element_type=F32)

    y_attn = jnp.dot(attn_ref[...], wao_ref[...], preferred_element_type=F32)

    pos = pos0 + i * ts + lax.broadcasted_iota(jnp.int32, (1, ts, 1), 1)
    pooled = []
    for gi, win in enumerate(POOL_WINDOWS):
        ls = slice(gi * LANES, (gi + 1) * LANES)
        cur = pext[:, POOL_HIST:POOL_HIST + ts, ls]
        ws = cur
        for k in range(1, win):
            ws = ws + pext[:, POOL_HIST - k:POOL_HIST - k + ts, ls]
        cnt = jnp.minimum(pos + 1, win).astype(F32)
        pg = (ws / cnt - cur).reshape(m, LANES).astype(BF16)
        pooled.append(jnp.dot(pg, pw_ref[gi], preferred_element_type=F32))
    y_pool = jnp.concatenate(pooled, axis=-1) * ps_ref[...]

    g0 = gate_ref[:, 0:d].astype(F32)
    g1 = gate_ref[:, d:2 * d].astype(F32)
    g2 = gate_ref[:, 2 * d:3 * d].astype(F32)
    merged = g0 * y_conv + g1 * y_attn + g2 * y_pool
    o_ref[...] = x_ref[...] + jnp.dot(merged.astype(BF16), wo_ref[...], preferred_element_type=F32)


def _mix(x, u, p, attn, gate, ust, pst, lw, nb, ts, pos0, into=None):
    (conv_w, conv_b, ln_g, ln_b, wco, wao, pool_w, pool_scale, wo) = lw
    b, t, c = u.shape
    d = x.shape[1]
    nt = t // ts
    has_hist = nt > 1
    assert t % ts == 0 and b % nb == 0 and (nb == 1 or not has_hist)
    out_rows, row0, prev = (x.shape[0], 0, None) if into is None else into
    assert row0 % (nb * ts) == 0 and out_rows % (nb * ts) == 0
    blk0 = row0 // (nb * ts)
    tok = lambda bi, ti: (bi * nt + ti, 0)
    tok_out = lambda bi, ti: (bi * nt + ti + blk0, 0)
    seq = lambda bi, ti: (bi, ti, 0)
    in_specs = [
        pl.BlockSpec((nb * ts, d), tok),
        pl.BlockSpec((nb, ts, c), seq),
        pl.BlockSpec((nb, ts, c), seq),
        pl.BlockSpec((nb * ts, c), tok),
        pl.BlockSpec((nb * ts, 3 * d), tok),
        pl.BlockSpec((nb, CONV_HIST, c), lambda bi, ti: (bi, 0, 0)),
        pl.BlockSpec((nb, POOL_HIST, c), lambda bi, ti: (bi, 0, 0)),
    ]
    args = [x, u, p, attn, gate, ust, pst]
    if has_hist:
        in_specs += [
            pl.BlockSpec((nb, CONV_HIST, c),
                         lambda bi, ti: (bi, jnp.maximum(ti * (ts // CONV_HIST) - 1, 0), 0)),
            pl.BlockSpec((nb, POOL_HIST, c),
                         lambda bi, ti: (bi, jnp.maximum(ti * (ts // POOL_HIST) - 1, 0), 0)),
        ]
        args += [u, p]
    consts = [conv_w, conv_b.reshape(1, c), ln_g.reshape(1, c), ln_b.reshape(1, c),
              wco, wao, pool_w, pool_scale.reshape(1, d), wo]
    in_specs += [_resident(a.shape) for a in consts]
    args += consts
    aliases = {}
    if prev is not None:
        aliases = {len(args): 0}
        in_specs.append(pl.BlockSpec(memory_space=pl.ANY))
        args.append(prev)
    return pl.pallas_call(
        functools.partial(_mix_body, has_hist=has_hist, has_prev=prev is not None, pos0=pos0),
        grid=(b // nb, nt),
        in_specs=in_specs,
        out_specs=pl.BlockSpec((nb * ts, d), tok_out),
        out_shape=jax.ShapeDtypeStruct((out_rows, d), F32),
        input_output_aliases=aliases,
        scratch_shapes=[pltpu.VMEM((nb, CONV_HIST + ts, c), F32),
                        pltpu.VMEM((nb, POOL_HIST + ts, c), F32),
                        pltpu.VMEM((nb, ts, c), F32),
                        pltpu.VMEM((7, nb, CONV_HIST + ts - 8, c), F32)],
        compiler_params=_params(("parallel", "parallel")),
        name="mix",
    )(*args)


def _swiglu_into(hn, w1_ref, w3_ref, a_scr, fc):
    f = a_scr.shape[1]
    for c0 in range(0, f, fc):
        n = min(fc, f - c0)
        h1 = jnp.dot(hn, w1_ref[:, c0:c0 + n], preferred_element_type=F32)
        h3 = jnp.dot(hn, w3_ref[:, c0:c0 + n], preferred_element_type=F32)
        a_scr[:, c0:c0 + n] = (h1 * jax.nn.sigmoid(h1) * h3).astype(BF16)


def _ffn_body(x_ref, g_ref, w1_ref, w3_ref, w2_ref, o_ref, a_scr, *, fc):
    x = x_ref[...]
    hn = _rms(x, g_ref[...]).astype(BF16)
    _swiglu_into(hn, w1_ref, w3_ref, a_scr, fc)
    o_ref[...] = x + jnp.dot(a_scr[...], w2_ref[...], preferred_element_type=F32)


def _ffn(x, g, w1, w3, w2, tm):
    n, d = x.shape
    assert n % tm == 0
    f = w1.shape[1]
    row = lambda i: (i, 0)
    return pl.pallas_call(
        functools.partial(_ffn_body, fc=512),
        grid=(n // tm,),
        in_specs=[pl.BlockSpec((tm, d), row), _resident((1, d)),
                  _resident(w1.shape), _resident(w3.shape), _resident(w2.shape)],
        out_specs=pl.BlockSpec((tm, d), row),
        out_shape=jax.ShapeDtypeStruct((n, d), F32),
        scratch_shapes=[pltpu.VMEM((tm, f), BF16)],
        compiler_params=_params(("parallel",)),
        name="ffn",
    )(x, g.reshape(1, d), w1, w3, w2)


MOE_CHUNK = 640
MOE_TILE = 256
MOE_SUB = 128
M_E1, M_E2, M_POS1, M_POS2, M_P1, M_P2 = range(6)
M_FIELDS = 8


def _route_top2(hn_f32, wr_ref, n_exp):
    xh, xl = _split_bf16(hn_f32)
    wh, wl = _split_bf16(wr_ref[...])
    logits = (jnp.dot(xh, wh, preferred_element_type=F32)
              + jnp.dot(xh, wl, preferred_element_type=F32)
              + jnp.dot(xl, wh, preferred_element_type=F32))
    lane = lax.broadcasted_iota(jnp.int32, logits.shape, 1)
    neg = jnp.float32(-jnp.inf)
    l1 = jnp.where(lane < n_exp, logits, neg)
    m1 = jnp.max(l1, axis=-1, keepdims=True)
    i1 = jnp.min(jnp.where(l1 == m1, lane, LANES), axis=-1, keepdims=True)
    l2 = jnp.where(lane == i1, neg, l1)
    m2 = jnp.max(l2, axis=-1, keepdims=True)
    i2 = jnp.min(jnp.where(l2 == m2, lane, LANES), axis=-1, keepdims=True)
    e2 = jnp.exp(m2 - m1)
    return i1, i2, 1.0 / (1.0 + e2), e2 / (1.0 + e2)


def _route_body(x_ref, g_ref, wr_ref, hn_ref, meta_ref, metat_ref, cnt_ref, run_scr, *, n_exp, sub):
    @pl.when(pl.program_id(0) == 0)
    def _():
        run_scr[...] = jnp.zeros_like(run_scr)

    hn = _rms(x_ref[...], g_ref[...])
    hn_ref[...] = hn.astype(BF16)
    i1, i2, p1, p2 = _route_top2(hn, wr_ref, n_exp)
    m = hn.shape[0]
    lane = lax.broadcasted_iota(jnp.int32, (m, LANES), 1)
    sel1 = lane == i1
    sel2 = lane == i2
    assign = (sel1 | sel2).astype(BF16)
    row = lax.broadcasted_iota(jnp.int32, (sub, sub), 0)
    col = lax.broadcasted_iota(jnp.int32, (sub, sub), 1)
    before = (col < row).astype(BF16)
    run = run_scr[0:1, :]
    ranks = []
    for k, r0 in enumerate(range(0, m, sub)):
        a = assign[r0:r0 + sub, :]
        ranks.append(jnp.dot(before, a, preferred_element_type=F32) + run)
        run = run + jnp.sum(a.astype(F32), axis=0, keepdims=True)
        cnt_ref[k] = jnp.broadcast_to(run, cnt_ref.shape[1:])
    rank = jnp.concatenate(ranks, axis=0)
    run_scr[...] = jnp.broadcast_to(run, run_scr.shape)
    pos1 = jnp.sum(jnp.where(sel1, rank, 0.0), axis=-1, keepdims=True)
    pos2 = jnp.sum(jnp.where(sel2, rank, 0.0), axis=-1, keepdims=True)
    fields = {M_E1: i1.astype(F32), M_E2: i2.astype(F32), M_POS1: pos1, M_POS2: pos2,
              M_P1: p1, M_P2: p2}
    meta = jnp.zeros((m, LANES), F32)
    for j, val in fields.items():
        meta = jnp.where(lane == j, val, meta)
    meta_ref[...] = meta
    metat_ref[...] = meta.T[0:M_FIELDS, :]


def _route(x, g, wr, chunk, sub):
    n, d = x.shape
    n_exp = wr.shape[1]
    wr_pad = jnp.zeros((d, LANES), F32).at[:, :n_exp].set(wr)
    assert n % chunk == 0 and chunk % sub == 0
    nc = n // chunk
    spc = chunk // sub
    row = lambda c: (c, 0)
    return pl.pallas_call(
        functools.partial(_route_body, n_exp=n_exp, sub=sub),
        grid=(nc,),
        in_specs=[pl.BlockSpec((chunk, d), row), _resident((1, d)), _resident((d, LANES))],
        out_specs=[pl.BlockSpec((chunk, d), row), pl.BlockSpec((chunk, LANES), row),
                   pl.BlockSpec((M_FIELDS, chunk), lambda c: (0, c)),
                   pl.BlockSpec((spc, 8, LANES), lambda c: (c, 0, 0))],
        out_shape=[jax.ShapeDtypeStruct((n, d), BF16),
                   jax.ShapeDtypeStruct((n, LANES), F32),
                   jax.ShapeDtypeStruct((M_FIELDS, n), F32),
                   jax.ShapeDtypeStruct((n // sub, 8, LANES), F32)],
        scratch_shapes=[pltpu.VMEM((8, LANES), F32)],
        compiler_params=_params(("arbitrary",)),
        name="moe_route",
    )(x, g.reshape(1, d), wr_pad)


def _gather_body(ptile, pchunk, pfirst, pvalid, texp, trank0, hn_ref, metat_ref, o_ref, acc_scr):
    i = pl.program_id(0)

    @pl.when(pvalid[i] == 1)
    def _():
        r = ptile[i]
        e = texp[r]
        mt = metat_ref[...]
        e1 = mt[M_E1:M_E1 + 1, :].astype(jnp.int32)
        e2 = mt[M_E2:M_E2 + 1, :].astype(jnp.int32)
        pos = jnp.where(e1 == e, mt[M_POS1:M_POS1 + 1, :],
                        jnp.where(e2 == e, mt[M_POS2:M_POS2 + 1, :], -1.0))
        slot = pos.astype(jnp.int32) - trank0[r]
        rows = lax.broadcasted_iota(jnp.int32, (o_ref.shape[0], slot.shape[1]), 0)
        onehot = (rows == slot).astype(BF16)
        res = jnp.dot(onehot, hn_ref[...], preferred_element_type=F32)

        @pl.when(pfirst[i] == 1)
        def _():
            acc_scr[...] = res

        @pl.when(pfirst[i] == 0)
        def _():
            acc_scr[...] += res

        o_ref[...] = acc_scr[...].astype(BF16)


def _expert_body(texp, nvalid, xs_ref, w1_ref, w3_ref, w2_ref, y_ref, a_scr, *, fc):
    @pl.when(pl.program_id(0) < nvalid[0])
    def _():
        _swiglu_into(xs_ref[...], w1_ref, w3_ref, a_scr, fc)
        y_ref[...] = jnp.dot(a_scr[...], w2_ref[...], preferred_element_type=F32).astype(BF16)


def _combine_body(ptile, pchunk, pfirst, plast, pvalid, pbits, texp, trank0,
                  x_ref, meta_ref, y_ref, gf_ref, o_ref, acc_scr, *, final_norm, sub):
    i = pl.program_id(0)

    @pl.when(pvalid[i] == 1)
    def _():
        r = ptile[i]
        e = texp[r]
        m = meta_ref[...]
        is1 = m[:, M_E1:M_E1 + 1].astype(jnp.int32) == e
        is2 = m[:, M_E2:M_E2 + 1].astype(jnp.int32) == e
        pos = jnp.where(is1, m[:, M_POS1:M_POS1 + 1],
                        jnp.where(is2, m[:, M_POS2:M_POS2 + 1], -1.0))
        gate = jnp.where(is1, m[:, M_P1:M_P1 + 1], jnp.where(is2, m[:, M_P2:M_P2 + 1], 0.0))
        slot = pos.astype(jnp.int32) - trank0[r]

        @pl.when(pfirst[i] == 1)
        def _():
            acc_scr[...] = jnp.zeros_like(acc_scr)

        cols = lax.broadcasted_iota(jnp.int32, (sub, y_ref.shape[0]), 1)
        for k, r0 in enumerate(range(0, m.shape[0], sub)):
            @pl.when(((pbits[i] >> k) & 1) == 1)
            def _(r0=r0):
                onehot = (cols == slot[r0:r0 + sub, :]).astype(BF16)
                acc_scr[r0:r0 + sub, :] += gate[r0:r0 + sub, :] * jnp.dot(
                    onehot, y_ref[...], preferred_element_type=F32)

        @pl.when(plast[i] == 1)
        def _():
            out = x_ref[...] + acc_scr[...]
            if final_norm:
                out = _rms(out, gf_ref[...])
            o_ref[...] = out


def _pair_list(hit, npairs):
    nb = hit.shape[1]
    idx = jnp.nonzero(hit.reshape(-1), size=npairs, fill_value=-1)[0].astype(jnp.int32)
    valid = idx >= 0
    count = jnp.sum(valid.astype(jnp.int32))
    idx = jnp.where(valid, idx, idx[jnp.maximum(count - 1, 0)])
    a = idx // nb
    b = idx % nb
    off = jnp.full((1,), -1, jnp.int32)
    first = valid & (a != jnp.concatenate([off, a[:-1]]))
    nxt_valid = jnp.concatenate([valid[1:], jnp.zeros((1,), bool)])
    last = valid & ((a != jnp.concatenate([a[1:], off])) | ~nxt_valid)
    i32 = lambda v: v.astype(jnp.int32)
    return idx, a, b, i32(first), i32(last), i32(valid)


def _moe(x, g, wr, w1, w3, w2, gf, final_norm, chunk=MOE_CHUNK, tg=MOE_TILE, sub=MOE_SUB):
    n, d = x.shape
    n_exp, _, f = w1.shape
    nc = n // chunk
    spc = chunk // sub
    nt = (TOP_K * n + n_exp * (tg - 1)) // tg
    npairs = nt + n_exp * nc

    hn, meta, metat, cnt = _route(x, g, wr, chunk, sub)

    after = cnt[:, 0, :n_exp].astype(jnp.int32)
    cb = jnp.concatenate([jnp.zeros((1, n_exp), jnp.int32), after], axis=0)
    tiles_e = (cb[-1] + tg - 1) // tg
    tend = jnp.cumsum(tiles_e)
    nvalid = tend[-1]
    r = jnp.arange(nt, dtype=jnp.int32)
    rc = jnp.minimum(r, nvalid - 1)
    texp = jnp.sum((rc[:, None] >= tend[None, :]).astype(jnp.int32), axis=1)
    trank0 = (rc - (tend - tiles_e)[texp]) * tg
    lo = cb[:-1][:, texp].T
    hi = cb[1:][:, texp].T
    sub_hit = ((r < nvalid)[:, None] & (hi > lo)
               & (lo < trank0[:, None] + tg) & (hi > trank0[:, None])).reshape(nt, nc, spc)
    hit = jnp.any(sub_hit, axis=2)
    bits = jnp.sum(sub_hit.astype(jnp.int32) << jnp.arange(spc, dtype=jnp.int32), axis=2)
    _, g_tile, g_chunk, g_first, _, g_valid = _pair_list(hit, npairs)
    c_idx, c_chunk, c_tile, c_first, c_last, c_valid = _pair_list(hit.T, npairs)
    c_bits = bits.T.reshape(-1)[c_idx] * c_valid
    nvalid1 = nvalid.reshape(1).astype(jnp.int32)

    xs = pl.pallas_call(
        _gather_body,
        grid_spec=pltpu.PrefetchScalarGridSpec(
            num_scalar_prefetch=6, grid=(npairs,),
            in_specs=[pl.BlockSpec((chunk, d), lambda i, pt, pc, *_: (pc[i], 0)),
                      pl.BlockSpec((M_FIELDS, chunk), lambda i, pt, pc, *_: (0, pc[i]))],
            out_specs=pl.BlockSpec((tg, d), lambda i, pt, *_: (pt[i], 0)),
            scratch_shapes=[pltpu.VMEM((tg, d), F32)]),
        out_shape=jax.ShapeDtypeStruct((nt * tg, d), BF16),
        compiler_params=_params(("arbitrary",)),
        name="moe_gather",
    )(g_tile, g_chunk, g_first, g_valid, texp, trank0, hn, metat)

    tile = lambda i, te, nv: (jnp.minimum(i, nv[0] - 1), 0)
    wexp = lambda i, te, nv: (te[i], 0, 0)
    y = pl.pallas_call(
        functools.partial(_expert_body, fc=512),
        grid_spec=pltpu.PrefetchScalarGridSpec(
            num_scalar_prefetch=2, grid=(nt,),
            in_specs=[pl.BlockSpec((tg, d), tile), pl.BlockSpec((None, d, f), wexp),
                      pl.BlockSpec((None, d, f), wexp), pl.BlockSpec((None, f, d), wexp)],
            out_specs=pl.BlockSpec((tg, d), tile),
            scratch_shapes=[pltpu.VMEM((tg, f), BF16)]),
        out_shape=jax.ShapeDtypeStruct((nt * tg, d), BF16),
        compiler_params=_params(("arbitrary",)),
        name="moe_expert",
    )(texp, nvalid1, xs, w1, w3, w2)

    return pl.pallas_call(
        functools.partial(_combine_body, final_norm=final_norm, sub=sub),
        grid_spec=pltpu.PrefetchScalarGridSpec(
            num_scalar_prefetch=8, grid=(npairs,),
            in_specs=[pl.BlockSpec((chunk, d), lambda i, pt, pc, *_: (pc[i], 0)),
                      pl.BlockSpec((chunk, LANES), lambda i, pt, pc, *_: (pc[i], 0)),
                      pl.BlockSpec((tg, d), lambda i, pt, *_: (pt[i], 0)),
                      pl.BlockSpec((1, d), lambda i, *_: (0, 0))],
            out_specs=pl.BlockSpec((chunk, d), lambda i, pt, pc, *_: (pc[i], 0)),
            scratch_shapes=[pltpu.VMEM((chunk, d), F32)]),
        out_shape=jax.ShapeDtypeStruct((n, d), F32),
        compiler_params=_params(("arbitrary",)),
        name="moe_combine",
    )(c_tile, c_chunk, c_first, c_last, c_valid, c_bits, texp, trank0,
      x, meta, y, gf.reshape(1, d))


def _norm_body(x_ref, g_ref, o_ref):
    o_ref[...] = _rms(x_ref[...], g_ref[...])


def _final_norm(x, g, tm):
    n, d = x.shape
    row = lambda i: (i, 0)
    return pl.pallas_call(
        _norm_body,
        grid=(n // tm,),
        in_specs=[pl.BlockSpec((tm, d), row), _resident((1, d))],
        out_specs=pl.BlockSpec((tm, d), row),
        out_shape=jax.ShapeDtypeStruct((n, d), F32),
        compiler_params=_params(("parallel",)),
        name="final_norm",
    )(x, g.reshape(1, d))


def _pad_front(state, rows):
    b, r, c = state.shape
    return jnp.concatenate([jnp.zeros((b, rows - r, c), state.dtype), state], axis=1)


def kernel(x_prompt, x_sample, cache_k, cache_v, state_conv, state_pool, page_table, norm_mix, w_in, conv_w, conv_b, conv_ln_g, conv_ln_b, w_conv_out, w_attn_out, sb_bias, pool_w, pool_scale, w_o, norm_ffn, ffn_w1, ffn_w3, ffn_w2, moe_router, moe_w1, moe_w3, moe_w2, norm_final):
    bp, tp, d = x_prompt.shape
    bs, tsm, _ = x_sample.shape
    depth = w_in.shape[0]
    a = N_HEADS * HEAD_DIM
    n_pages = page_table.shape[1]
    past = n_pages * PAGE_SIZE
    cshape = (cache_k.shape[0], cache_k.shape[1], a, PAGE_SIZE)
    ck = jnp.transpose(cache_k, (0, 1, 3, 4, 2)).reshape(cshape)
    cv = jnp.transpose(cache_v, (0, 1, 3, 4, 2)).reshape(cshape)

    xp = x_prompt.reshape(bp * tp, d)
    xs = x_sample.reshape(bs * tsm, d)
    outs = {k: [] for k in ("cp", "pp", "ks", "vs", "cs", "ps")}
    kv_prompt = None
    for l in range(depth):
        w_in_l = w_in[l].astype(BF16)
        lw = (conv_w[l], conv_b[l], conv_ln_g[l], conv_ln_b[l], w_conv_out[l].astype(BF16),
              w_attn_out[l].astype(BF16), pool_w[l].astype(BF16), pool_scale[l], w_o[l].astype(BF16))

        u, q, kt_p, vt_p, p, gate = _in_proj(xp, norm_mix[l], w_in_l, tm=512,
                                             kv_seq=(l, depth, bp, tp, kv_prompt))
        kv_prompt = (kt_p, vt_p)
        c = u.shape[1]
        attn = _attn_prompt(q.reshape(bp, tp, a), kt_p, vt_p, l, sb_bias[l])
        u3 = u.reshape(bp, tp, c)
        p3 = p.reshape(bp, tp, c)
        routed = l % 2 == 1
        n_all = bp * tp + bs * tsm
        xp = _mix(xp, u3, p3, attn.reshape(bp * tp, a), gate,
                  jnp.zeros((bp, CONV_HIST, c), F32), jnp.zeros((bp, POOL_HIST, c), F32),
                  lw, nb=1, ts=256, pos0=0, into=(n_all, 0, None) if routed else None)
        outs["cp"].append(u3[:, tp - (CONV_W - 1):])
        outs["pp"].append(p3[:, tp - (POOL_MAX - 1):])

        u, q, k, v, p, gate = _in_proj(xs, norm_mix[l], w_in_l, tm=bs * tsm)
        attn = _attn_sample(q.reshape(bs, tsm, a), k.reshape(bs, tsm, a), v.reshape(bs, tsm, a),
                            ck, cv, l, page_table, sb_bias[l])
        u3 = u.reshape(bs, tsm, c)
        p3 = p.reshape(bs, tsm, c)
        xs = _mix(xs, u3, p3, attn.reshape(bs * tsm, a), gate,
                  _pad_front(state_conv[l], CONV_HIST), _pad_front(state_pool[l], POOL_HIST),
                  lw, nb=bs, ts=tsm, pos0=past, into=(n_all, bp * tp, xp) if routed else None)
        outs["ks"].append(k.reshape(bs, tsm, N_HEADS, HEAD_DIM))
        outs["vs"].append(v.reshape(bs, tsm, N_HEADS, HEAD_DIM))
        outs["cs"].append(jnp.concatenate([state_conv[l], u3], axis=1)[:, -(CONV_W - 1):])
        outs["ps"].append(jnp.concatenate([state_pool[l], p3], axis=1)[:, -(POOL_MAX - 1):])

        i = l // 2
        last = l == depth - 1
        if l % 2 == 0:
            w1, w3, w2 = ffn_w1[i].astype(BF16), ffn_w3[i].astype(BF16), ffn_w2[i].astype(BF16)
            xp = _ffn(xp, norm_ffn[l], w1, w3, w2, tm=512)
            xs = _ffn(xs, norm_ffn[l], w1, w3, w2, tm=bs * tsm)
            if last:
                xp = _final_norm(xp, norm_final, tm=512)
                xs = _final_norm(xs, norm_final, tm=bs * tsm)
        else:
            w1, w3, w2 = moe_w1[i].astype(BF16), moe_w3[i].astype(BF16), moe_w2[i].astype(BF16)
            x_all = _moe(xs, norm_ffn[l], moe_router[i], w1, w3, w2, norm_final, last)
            xp, xs = x_all[:bp * tp], x_all[bp * tp:]

    k_prompt, v_prompt = (
        jnp.transpose(t.reshape(depth, bp, N_HEADS, HEAD_DIM, tp), (0, 1, 4, 2, 3))
        for t in kv_prompt)
    return (xp.reshape(bp, tp, d), xs.reshape(bs, tsm, d),
            k_prompt, v_prompt, jnp.stack(outs["cp"]), jnp.stack(outs["pp"]),
            jnp.stack(outs["ks"]), jnp.stack(outs["vs"]), jnp.stack(outs["cs"]), jnp.stack(outs["ps"]))
```

```python
import functools

import jax
import jax.numpy as jnp
from jax import lax
from jax.experimental import pallas as pl
from jax.experimental.pallas import tpu as pltpu

F32 = jnp.float32
BF16 = jnp.bfloat16
EPS = 1e-6
LOG2E = 1.4426950408889634

CONV_W = 31
POOL_WINDOWS = (2, 4, 8, 16)
POOL_MAX = 16
N_HEADS = 8
HEAD_DIM = 64
PAGE_SIZE = 128
TOP_K = 2
CONV_HIST = 32
POOL_HIST = 16
LANES = 128
VMEM_LIMIT = 56 * 1024 * 1024


def _params(sem):
    return pltpu.CompilerParams(dimension_semantics=sem, vmem_limit_bytes=VMEM_LIMIT)


def _resident(shape):
    zeros = (0,) * len(shape)
    return pl.BlockSpec(shape, lambda *_: zeros, pipeline_mode=pl.Buffered(1))


def _rms(x, g):
    return x * lax.rsqrt(jnp.mean(x * x, axis=-1, keepdims=True) + EPS) * g


def _softplus(z):
    return jnp.maximum(z, 0.0) + jnp.log(1.0 + jnp.exp2(jnp.abs(z) * -LOG2E))


def _split_bf16(x):
    hi = x.astype(BF16)
    return hi, (x - hi.astype(F32)).astype(BF16)


def _in_proj_body(x_ref, g_ref, w_ref, *rest, q_scale, kv_t):
    if kv_t:
        stage = rest[-1]
        rest = rest[:-1]
    u_ref, q_ref, k_ref, v_ref, p_ref, gate_ref = rest[-6:]

    def put_kv(ref, val):
        if kv_t:
            stage[...] = val
            ref[...] = stage[...].T
        else:
            ref[...] = val
    hn = _rms(x_ref[...], g_ref[...]).astype(BF16)
    cd = u_ref.shape[1]
    ad = q_ref.shape[1]
    pd = p_ref.shape[1]

    def proj(c0, n):
        return jnp.dot(hn, w_ref[:, c0:c0 + n], preferred_element_type=F32)

    c = 0
    a = proj(c, cd)
    b = proj(c + cd, cd)
    u_ref[...] = a * jax.nn.sigmoid(b)
    c += 2 * cd
    q_ref[...] = (proj(c, ad) * q_scale).astype(BF16)
    c += ad
    put_kv(k_ref, proj(c, ad))
    c += ad
    put_kv(v_ref, proj(c, ad))
    c += ad
    p_ref[...] = proj(c, pd)
    c += pd
    ng = gate_ref.shape[1]
    step = 512
    for j in range(0, ng, step):
        gate_ref[:, j:j + step] = jax.nn.sigmoid(proj(c + j, step)).astype(BF16)


def _in_proj(x, g, w, tm, kv_seq=None):
    n, d = x.shape
    assert n % tm == 0
    cd = ad = pd = 512
    ng = w.shape[1] - 2 * cd - 3 * ad - pd
    row = lambda i: (i, 0)
    kv_shape, kv_spec, prev = (n, ad), pl.BlockSpec((tm, ad), row), ()
    if kv_seq is not None:
        layer, depth, batch, seq, prev = kv_seq
        assert seq % tm == 0 and batch * seq == n
        tpb = seq // tm
        kv_shape = (depth, batch, ad, seq)
        kv_spec = pl.BlockSpec((None, None, ad, tm), lambda i: (layer, i // tpb, 0, i % tpb))
        prev = () if prev is None else tuple(prev)
    outs = [
        jax.ShapeDtypeStruct((n, cd), F32),
        jax.ShapeDtypeStruct((n, ad), BF16),
        jax.ShapeDtypeStruct(kv_shape, F32),
        jax.ShapeDtypeStruct(kv_shape, F32),
        jax.ShapeDtypeStruct((n, pd), F32),
        jax.ShapeDtypeStruct((n, ng), BF16),
    ]
    out_specs = [pl.BlockSpec((tm, o.shape[-1]), row) for o in outs]
    out_specs[2] = out_specs[3] = kv_spec
    return pl.pallas_call(
        functools.partial(_in_proj_body, q_scale=HEAD_DIM ** -0.5, kv_t=kv_seq is not None),
        grid=(n // tm,),
        in_specs=([pl.BlockSpec((tm, d), row), _resident((1, d)), _resident(w.shape)]
                  + [pl.BlockSpec(memory_space=pl.ANY)] * len(prev)),
        out_specs=out_specs,
        out_shape=outs,
        input_output_aliases={3 + j: 2 + j for j in range(len(prev))},
        scratch_shapes=[pltpu.VMEM((tm, ad), F32)] if kv_seq is not None else [],
        compiler_params=_params(("parallel",)),
        name="in_proj",
    )(x, g.reshape(1, d), w, *prev)


def _attn_prompt_body(bias_ref, q_ref, kt_ref, vt_ref, o_ref, k_scr, vt_scr, acc_scr, *, tq, hp):
    t_len = q_ref.shape[0]
    nb = t_len // tq
    hd = HEAD_DIM
    g = pl.program_id(1)
    for j in range(nb):
        k_scr[j] = kt_ref[:, j * tq:(j + 1) * tq].T.astype(BF16)
        vt_scr[j] = vt_ref[:, j * tq:(j + 1) * tq].astype(BF16)
    key = lax.broadcasted_iota(jnp.int32, (tq, tq), 0)
    qry = lax.broadcasted_iota(jnp.int32, (tq, tq), 1)
    valid = key < qry
    tri = (qry >= key).astype(BF16)

    biases = [bias_ref[g * hp + hh] for hh in range(hp)]

    def block(qi, kj, cs_in, diag):
        q_rows = pl.ds(pl.multiple_of(qi * tq, tq), tq)
        heads = [slice(hh * hd, (hh + 1) * hd) for hh in range(hp)]
        zs = [lax.dot_general(k_scr[kj, :, hs], q_ref[q_rows, hs],
                              (((1,), (1,)), ((), ())), preferred_element_type=F32) + biases[hh]
              for hh, hs in enumerate(heads)]
        sps = [_softplus(z) for z in zs]
        if diag:
            sps = [jnp.where(valid, sp, 0.0) for sp in sps]
        css = [jnp.dot(tri, sp.astype(BF16), preferred_element_type=F32) for sp in sps]
        ws = [jnp.exp(z - cs - c) for z, cs, c in zip(zs, css, cs_in)]
        if diag:
            ws = [jnp.where(valid, w, 0.0) for w in ws]
        pvs = [jnp.dot(vt_scr[kj, hs, :], w.astype(BF16), preferred_element_type=F32)
               for hs, w in zip(heads, ws)]
        for hs, pv in zip(heads, pvs):
            if diag:
                acc_scr[qi, hs, :] = pv
            else:
                acc_scr[qi, hs, :] += pv
        return tuple(c + cs[0:1, :] for c, cs in zip(cs_in, css))

    def q_loop(qi, carry):
        c0 = block(qi, qi, (jnp.zeros((1, tq), F32),) * hp, True)
        lax.fori_loop(0, qi, lambda i, c: block(qi, qi - 1 - i, c, False), c0)
        return carry

    lax.fori_loop(0, nb, q_loop, 0)
    for j in range(nb):
        o_ref[j * tq:(j + 1) * tq, :] = acc_scr[j].T.astype(BF16)


def _attn_prompt(q, kt, vt, layer, bias, tq=256, hp=8):
    b, t, a = q.shape
    w = hp * HEAD_DIM
    assert t % tq == 0 and a % w == 0
    spec = pl.BlockSpec((None, t, w), lambda bi, gi: (bi, 0, gi))
    spec_t = pl.BlockSpec((None, None, w, t), lambda bi, gi: (layer, bi, gi, 0))
    nb = t // tq
    return pl.pallas_call(
        functools.partial(_attn_prompt_body, tq=tq, hp=hp),
        grid=(b, a // w),
        in_specs=[pl.BlockSpec(memory_space=pltpu.SMEM), spec, spec_t, spec_t],
        out_specs=spec,
        out_shape=jax.ShapeDtypeStruct((b, t, a), BF16),
        scratch_shapes=[pltpu.VMEM((nb, tq, w), BF16), pltpu.VMEM((nb, w, tq), BF16),
                        pltpu.VMEM((nb, w, tq), F32)],
        compiler_params=_params(("parallel", "parallel")),
        name="attn_prompt",
    )(bias, q, kt, vt)


def _attn_sample_body(pt_ref, bias_ref, q_ref, kn_ref, vn_ref, *rest, pp):
    kpg = rest[:pp]
    vpg = rest[pp:2 * pp]
    o_ref = rest[2 * pp]
    qbd_scr, bcol_scr, c_scr, acc_scr, pad_scr = rest[2 * pp + 1:]
    s = pl.program_id(1)
    t_new, a = q_ref.shape
    hd = HEAD_DIM
    nh = a // hd
    nq = nh * t_new

    def chunk(kt, vt, masked):
        r = kt.shape[1]
        sb = min(r, 2 * PAGE_SIZE)
        nblk = r // sb
        summed = lax.broadcasted_iota(jnp.int32, (sb, sb), 0)
        at = lax.broadcasted_iota(jnp.int32, (sb, sb), 1)
        tri = (summed >= at).astype(BF16)
        z = jnp.dot(qbd_scr[...], kt, preferred_element_type=F32) + bcol_scr[...]
        sp = _softplus(z)
        if masked:
            key = lax.broadcasted_iota(jnp.int32, z.shape, 1)
            qry = lax.broadcasted_iota(jnp.int32, z.shape, 0) % t_new
            ok = key < qry
            sp = jnp.where(ok, sp, 0.0)
        sp_st = jnp.concatenate([sp[:, j * sb:(j + 1) * sb] for j in range(nblk)], axis=0)
        cs_st = jnp.dot(sp_st.astype(BF16), tri, preferred_element_type=F32)
        run = c_scr[...]
        ws = [None] * nblk
        for j in reversed(range(nblk)):
            cs = cs_st[j * nq:(j + 1) * nq, :]
            ws[j] = jnp.exp(z[:, j * sb:(j + 1) * sb] - cs - run)
            run = run + cs[:, 0:1]
        c_scr[...] = run
        w = jnp.concatenate(ws, axis=1)
        if masked:
            w = jnp.where(ok, w, 0.0)
        acc_scr[...] += lax.dot_general(w.astype(BF16), vt, (((1,), (1,)), ((), ())),
                                        preferred_element_type=F32)

    @pl.when(s == 0)
    def _():
        qt = jnp.concatenate([q_ref[...].astype(F32)] * nh, axis=0)
        hq = lax.broadcasted_iota(jnp.int32, qt.shape, 0)
        col = lax.broadcasted_iota(jnp.int32, qt.shape, 1)
        qbd_scr[...] = jnp.where(hq // t_new == col // hd, qt, 0.0).astype(BF16)
        row = lax.broadcasted_iota(jnp.int32, (nq, 1), 0)
        bcol = jnp.zeros((nq, 1), F32)
        for h in range(nh):
            bcol = jnp.where(row // t_new == h, bias_ref[h], bcol)
        bcol_scr[...] = bcol
        c_scr[...] = jnp.zeros_like(c_scr)
        acc_scr[...] = jnp.zeros_like(acc_scr)
        pad_scr[...] = jnp.zeros_like(pad_scr)
        pad_scr[0:t_new, :] = kn_ref[...]
        knt = pad_scr[...].T.astype(BF16)
        pad_scr[0:t_new, :] = vn_ref[...]
        vnt = pad_scr[...].T.astype(BF16)
        chunk(knt, vnt, True)

    kt = jnp.concatenate([kpg[i][...].astype(BF16) for i in reversed(range(pp))], axis=1)
    vt = jnp.concatenate([vpg[i][...].astype(BF16) for i in reversed(range(pp))], axis=1)
    chunk(kt, vt, False)

    @pl.when(s == pl.num_programs(1) - 1)
    def _():
        col = lax.broadcasted_iota(jnp.int32, (t_new, a), 1)
        out = jnp.zeros((t_new, a), F32)
        for h in range(nh):
            out = out + jnp.where(col // hd == h, acc_scr[h * t_new:(h + 1) * t_new, :], 0.0)
        o_ref[...] = out.astype(BF16)


def _attn_sample(q, kn, vn, cache_kt, cache_vt, layer, page_table, bias, pp=32):
    b, t, a = q.shape
    n_pages = page_table.shape[1]
    assert n_pages % pp == 0 and t <= PAGE_SIZE
    nq = (a // HEAD_DIM) * t
    new_spec = pl.BlockSpec((None, t, a), lambda bi, si, pt: (bi, 0, 0))

    def page_spec(i):
        return pl.BlockSpec(
            (None, None, a, PAGE_SIZE),
            lambda bi, si, pt: (layer, pt[bi, n_pages - 1 - (si * pp + i)], 0, 0))

    grid_spec = pltpu.PrefetchScalarGridSpec(
        num_scalar_prefetch=1,
        grid=(b, n_pages // pp),
        in_specs=([pl.BlockSpec(memory_space=pltpu.SMEM), new_spec, new_spec, new_spec]
                  + [page_spec(i) for i in range(pp)] * 2),
        out_specs=new_spec,
        scratch_shapes=[
            pltpu.VMEM((nq, a), BF16),
            pltpu.VMEM((nq, 1), F32),
            pltpu.VMEM((nq, 1), F32),
            pltpu.VMEM((nq, a), F32),
            pltpu.VMEM((PAGE_SIZE, a), F32),
        ],
    )
    return pl.pallas_call(
        functools.partial(_attn_sample_body, pp=pp),
        grid_spec=grid_spec,
        out_shape=jax.ShapeDtypeStruct((b, t, a), BF16),
        compiler_params=_params(("parallel", "arbitrary")),
        name="attn_sample",
    )(page_table, bias, q, kn, vn, *([cache_kt] * pp), *([cache_vt] * pp))


def _mix_body(*refs, has_hist, has_prev, pos0):
    if has_prev:
        n_in = 16 + (2 if has_hist else 0)
        refs = refs[:n_in] + refs[n_in + 1:]
    if has_hist:
        (x_ref, u_ref, p_ref, attn_ref, gate_ref, ust_ref, pst_ref, uh_ref, ph_ref,
         cw_ref, cb_ref, lg_ref, lb_ref, wco_ref, wao_ref, pw_ref, ps_ref, wo_ref,
         o_ref, uext, pext, ycv, ush) = refs
    else:
        (x_ref, u_ref, p_ref, attn_ref, gate_ref, ust_ref, pst_ref,
         cw_ref, cb_ref, lg_ref, lb_ref, wco_ref, wao_ref, pw_ref, ps_ref, wo_ref,
         o_ref, uext, pext, ycv, ush) = refs
    i = pl.program_id(1)
    nb, ts, c = u_ref.shape
    m = nb * ts
    d = x_ref.shape[1]

    if has_hist:
        first = i == 0
        uext[:, 0:CONV_HIST, :] = jnp.where(first, ust_ref[...], uh_ref[...])
        pext[:, 0:POOL_HIST, :] = jnp.where(first, pst_ref[...], ph_ref[...])
    else:
        uext[:, 0:CONV_HIST, :] = ust_ref[...]
        pext[:, 0:POOL_HIST, :] = pst_ref[...]
    uext[:, CONV_HIST:CONV_HIST + ts, :] = u_ref[...]
    pext[:, POOL_HIST:POOL_HIST + ts, :] = p_ref[...]

    off = CONV_HIST - (CONV_W - 1)
    for b in range(1, 8):
        ush[b - 1] = uext[:, b:b + ush.shape[2], :]
    rc = min(ts, 64)
    for lg in range(c // LANES):
        ls = slice(lg * LANES, (lg + 1) * LANES)
        for r0 in range(0, ts, rc):
            acc = jnp.zeros((nb, rc, LANES), F32)
            for j in range(CONV_W):
                b = (off + j) % 8
                a = r0 + off + j - b
                win = uext[:, a:a + rc, ls] if b == 0 else ush[b - 1, :, a:a + rc, ls]
                acc = acc + cw_ref[j:j + 1, ls] * win
            ycv[:, r0:r0 + rc, ls] = acc

    yf = ycv[...].reshape(m, c) + cb_ref[...]
    mu = jnp.mean(yf, axis=-1, keepdims=True)
    yc = yf - mu
    var = jnp.mean(yc * yc, axis=-1, keepdims=True)
    yn = yc * lax.rsqrt(var + EPS) * lg_ref[...] + lb_ref[...]
    y_conv = jnp.dot((yn * jax.nn.sigmoid(yn)).astype(BF16), wco_ref[...],
                     preferred_element_type=F32)

    y_attn = jnp.dot(attn_ref[...], wao_ref[...], preferred_element_type=F32)

    pos = pos0 + i * ts + lax.broadcasted_iota(jnp.int32, (1, ts, 1), 1)
    pooled = []
    for gi, win in enumerate(POOL_WINDOWS):
        ls = slice(gi * LANES, (gi + 1) * LANES)
        cur = pext[:, POOL_HIST:POOL_HIST + ts, ls]
        ws = cur
        for k in range(1, win):
            ws = ws + pext[:, POOL_HIST - k:POOL_HIST - k + ts, ls]
        cnt = jnp.minimum(pos + 1, win).astype(F32)
        pg = (ws / cnt - cur).reshape(m, LANES).astype(BF16)
        pooled.append(jnp.dot(pg, pw_ref[gi], preferred_element_type=F32))
    y_pool = jnp.concatenate(pooled, axis=-1) * ps_ref[...]

    g0 = gate_ref[:, 0:d].astype(F32)
    g1 = gate_ref[:, d:2 * d].astype(F32)
    g2 = gate_ref[:, 2 * d:3 * d].astype(F32)
    merged = g0 * y_conv + g1 * y_attn + g2 * y_pool
    o_ref[...] = x_ref[...] + jnp.dot(merged.astype(BF16), wo_ref[...], preferred_element_type=F32)


def _mix(x, u, p, attn, gate, ust, pst, lw, nb, ts, pos0, into=None):
    (conv_w, conv_b, ln_g, ln_b, wco, wao, pool_w, pool_scale, wo) = lw
    b, t, c = u.shape
    d = x.shape[1]
    nt = t // ts
    has_hist = nt > 1
    assert t % ts == 0 and b % nb == 0 and (nb == 1 or not has_hist)
    out_rows, row0, prev = (x.shape[0], 0, None) if into is None else into
    assert row0 % (nb * ts) == 0 and out_rows % (nb * ts) == 0
    blk0 = row0 // (nb * ts)
    tok = lambda bi, ti: (bi * nt + ti, 0)
    tok_out = lambda bi, ti: (bi * nt + ti + blk0, 0)
    seq = lambda bi, ti: (bi, ti, 0)
    in_specs = [
        pl.BlockSpec((nb * ts, d), tok),
        pl.BlockSpec((nb, ts, c), seq),
        pl.BlockSpec((nb, ts, c), seq),
        pl.BlockSpec((nb * ts, c), tok),
        pl.BlockSpec((nb * ts, 3 * d), tok),
        pl.BlockSpec((nb, CONV_HIST, c), lambda bi, ti: (bi, 0, 0)),
        pl.BlockSpec((nb, POOL_HIST, c), lambda bi, ti: (bi, 0, 0)),
    ]
    args = [x, u, p, attn, gate, ust, pst]
    if has_hist:
        in_specs += [
            pl.BlockSpec((nb, CONV_HIST, c),
                         lambda bi, ti: (bi, jnp.maximum(ti * (ts // CONV_HIST) - 1, 0), 0)),
            pl.BlockSpec((nb, POOL_HIST, c),
                         lambda bi, ti: (bi, jnp.maximum(ti * (ts // POOL_HIST) - 1, 0), 0)),
        ]
        args += [u, p]
    consts = [conv_w, conv_b.reshape(1, c), ln_g.reshape(1, c), ln_b.reshape(1, c),
              wco, wao, pool_w, pool_scale.reshape(1, d), wo]
    in_specs += [_resident(a.shape) for a in consts]
    args += consts
    aliases = {}
    if prev is not None:
        aliases = {len(args): 0}
        in_specs.append(pl.BlockSpec(memory_space=pl.ANY))
        args.append(prev)
    return pl.pallas_call(
        functools.partial(_mix_body, has_hist=has_hist, has_prev=prev is not None, pos0=pos0),
        grid=(b // nb, nt),
        in_specs=in_specs,
        out_specs=pl.BlockSpec((nb * ts, d), tok_out),
        out_shape=jax.ShapeDtypeStruct((out_rows, d), F32),
        input_output_aliases=aliases,
        scratch_shapes=[pltpu.VMEM((nb, CONV_HIST + ts, c), F32),
                        pltpu.VMEM((nb, POOL_HIST + ts, c), F32),
                        pltpu.VMEM((nb, ts, c), F32),
                        pltpu.VMEM((7, nb, CONV_HIST + ts - 8, c), F32)],
        compiler_params=_params(("parallel", "parallel")),
        name="mix",
    )(*args)


def _swiglu_into(hn, w1_ref, w3_ref, a_scr, fc):
    f = a_scr.shape[1]
    for c0 in range(0, f, fc):
        n = min(fc, f - c0)
        h1 = jnp.dot(hn, w1_ref[:, c0:c0 + n], preferred_element_type=F32)
        h3 = jnp.dot(hn, w3_ref[:, c0:c0 + n], preferred_element_type=F32)
        a_scr[:, c0:c0 + n] = (h1 * jax.nn.sigmoid(h1) * h3).astype(BF16)


def _ffn_body(x_ref, g_ref, w1_ref, w3_ref, w2_ref, o_ref, a_scr, *, fc):
    x = x_ref[...]
    hn = _rms(x, g_ref[...]).astype(BF16)
    _swiglu_into(hn, w1_ref, w3_ref, a_scr, fc)
    o_ref[...] = x + jnp.dot(a_scr[...], w2_ref[...], preferred_element_type=F32)


def _ffn(x, g, w1, w3, w2, tm):
    n, d = x.shape
    assert n % tm == 0
    f = w1.shape[1]
    row = lambda i: (i, 0)
    return pl.pallas_call(
        functools.partial(_ffn_body, fc=512),
        grid=(n // tm,),
        in_specs=[pl.BlockSpec((tm, d), row), _resident((1, d)),
                  _resident(w1.shape), _resident(w3.shape), _resident(w2.shape)],
        out_specs=pl.BlockSpec((tm, d), row),
        out_shape=jax.ShapeDtypeStruct((n, d), F32),
        scratch_shapes=[pltpu.VMEM((tm, f), BF16)],
        compiler_params=_params(("parallel",)),
        name="ffn",
    )(x, g.reshape(1, d), w1, w3, w2)


MOE_CHUNK = 1280
MOE_TILE = 256
MOE_SUB = 256
M_E1, M_E2, M_POS1, M_POS2, M_P1, M_P2 = range(6)
M_FIELDS = 8


def _route_top2(hn_f32, wr_ref, n_exp):
    xh, xl = _split_bf16(hn_f32)
    wh, wl = _split_bf16(wr_ref[...])
    logits = (jnp.dot(xh, wh, preferred_element_type=F32)
              + jnp.dot(xh, wl, preferred_element_type=F32)
              + jnp.dot(xl, wh, preferred_element_type=F32))
    lane = lax.broadcasted_iota(jnp.int32, logits.shape, 1)
    neg = jnp.float32(-jnp.inf)
    l1 = jnp.where(lane < n_exp, logits, neg)
    m1 = jnp.max(l1, axis=-1, keepdims=True)
    i1 = jnp.min(jnp.where(l1 == m1, lane, LANES), axis=-1, keepdims=True)
    l2 = jnp.where(lane == i1, neg, l1)
    m2 = jnp.max(l2, axis=-1, keepdims=True)
    i2 = jnp.min(jnp.where(l2 == m2, lane, LANES), axis=-1, keepdims=True)
    e2 = jnp.exp(m2 - m1)
    return i1, i2, 1.0 / (1.0 + e2), e2 / (1.0 + e2)


def _route_body(x_ref, g_ref, wr_ref, hn_ref, meta_ref, metat_ref, cnt_ref, run_scr, *, n_exp, sub):
    @pl.when(pl.program_id(0) == 0)
    def _():
        run_scr[...] = jnp.zeros_like(run_scr)

    hn = _rms(x_ref[...], g_ref[...])
    hn_ref[...] = hn.astype(BF16)
    i1, i2, p1, p2 = _route_top2(hn, wr_ref, n_exp)
    m = hn.shape[0]
    lane = lax.broadcasted_iota(jnp.int32, (m, LANES), 1)
    sel1 = lane == i1
    sel2 = lane == i2
    assign = (sel1 | sel2).astype(BF16)
    row = lax.broadcasted_iota(jnp.int32, (sub, sub), 0)
    col = lax.broadcasted_iota(jnp.int32, (sub, sub), 1)
    before = (col < row).astype(BF16)
    run = run_scr[0:1, :]
    ranks = []
    for k, r0 in enumerate(range(0, m, sub)):
        a = assign[r0:r0 + sub, :]
        ranks.append(jnp.dot(before, a, preferred_element_type=F32) + run)
        run = run + jnp.sum(a.astype(F32), axis=0, keepdims=True)
        cnt_ref[k] = jnp.broadcast_to(run, cnt_ref.shape[1:])
    rank = jnp.concatenate(ranks, axis=0)
    run_scr[...] = jnp.broadcast_to(run, run_scr.shape)
    pos1 = jnp.sum(jnp.where(sel1, rank, 0.0), axis=-1, keepdims=True)
    pos2 = jnp.sum(jnp.where(sel2, rank, 0.0), axis=-1, keepdims=True)
    fields = {M_E1: i1.astype(F32), M_E2: i2.astype(F32), M_POS1: pos1, M_POS2: pos2,
              M_P1: p1, M_P2: p2}
    meta = jnp.zeros((m, LANES), F32)
    for j, val in fields.items():
        meta = jnp.where(lane == j, val, meta)
    meta_ref[...] = meta
    metat_ref[...] = meta.T[0:M_FIELDS, :]


def _route(x, g, wr, chunk, sub):
    n, d = x.shape
    n_exp = wr.shape[1]
    wr_pad = jnp.zeros((d, LANES), F32).at[:, :n_exp].set(wr)
    assert n % chunk == 0 and chunk % sub == 0
    nc = n // chunk
    spc = chunk // sub
    row = lambda c: (c, 0)
    return pl.pallas_call(
        functools.partial(_route_body, n_exp=n_exp, sub=sub),
        grid=(nc,),
        in_specs=[pl.BlockSpec((chunk, d), row), _resident((1, d)), _resident((d, LANES))],
        out_specs=[pl.BlockSpec((chunk, d), row), pl.BlockSpec((chunk, LANES), row),
                   pl.BlockSpec((M_FIELDS, chunk), lambda c: (0, c)),
                   pl.BlockSpec((spc, 8, LANES), lambda c: (c, 0, 0))],
        out_shape=[jax.ShapeDtypeStruct((n, d), BF16),
                   jax.ShapeDtypeStruct((n, LANES), F32),
                   jax.ShapeDtypeStruct((M_FIELDS, n), F32),
                   jax.ShapeDtypeStruct((n // sub, 8, LANES), F32)],
        scratch_shapes=[pltpu.VMEM((8, LANES), F32)],
        compiler_params=_params(("arbitrary",)),
        name="moe_route",
    )(x, g.reshape(1, d), wr_pad)


def _gather_body(ptile, pchunk, pfirst, pvalid, texp, trank0, hn_ref, metat_ref, o_ref, acc_scr):
    i = pl.program_id(0)

    @pl.when(pvalid[i] == 1)
    def _():
        r = ptile[i]
        e = texp[r]
        mt = metat_ref[...]
        e1 = mt[M_E1:M_E1 + 1, :].astype(jnp.int32)
        e2 = mt[M_E2:M_E2 + 1, :].astype(jnp.int32)
        pos = jnp.where(e1 == e, mt[M_POS1:M_POS1 + 1, :],
                        jnp.where(e2 == e, mt[M_POS2:M_POS2 + 1, :], -1.0))
        slot = pos.astype(jnp.int32) - trank0[r]
        rows = lax.broadcasted_iota(jnp.int32, (o_ref.shape[0], slot.shape[1]), 0)
        onehot = (rows == slot).astype(BF16)
        res = jnp.dot(onehot, hn_ref[...], preferred_element_type=F32)

        @pl.when(pfirst[i] == 1)
        def _():
            acc_scr[...] = res

        @pl.when(pfirst[i] == 0)
        def _():
            acc_scr[...] += res

        o_ref[...] = acc_scr[...].astype(BF16)


def _expert_body(texp, nvalid, xs_ref, w1_ref, w3_ref, w2_ref, y_ref, a_scr, *, fc):
    @pl.when(pl.program_id(0) < nvalid[0])
    def _():
        _swiglu_into(xs_ref[...], w1_ref, w3_ref, a_scr, fc)
        y_ref[...] = jnp.dot(a_scr[...], w2_ref[...], preferred_element_type=F32).astype(BF16)


def _combine_body(ptile, pchunk, pfirst, plast, pvalid, pbits, texp, trank0,
                  x_ref, meta_ref, y_ref, gf_ref, o_ref, acc_scr, *, final_norm, sub):
    i = pl.program_id(0)

    @pl.when(pvalid[i] == 1)
    def _():
        r = ptile[i]
        e = texp[r]
        m = meta_ref[...]
        is1 = m[:, M_E1:M_E1 + 1].astype(jnp.int32) == e
        is2 = m[:, M_E2:M_E2 + 1].astype(jnp.int32) == e
        pos = jnp.where(is1, m[:, M_POS1:M_POS1 + 1],
                        jnp.where(is2, m[:, M_POS2:M_POS2 + 1], -1.0))
        gate = jnp.where(is1, m[:, M_P1:M_P1 + 1], jnp.where(is2, m[:, M_P2:M_P2 + 1], 0.0))
        slot = pos.astype(jnp.int32) - trank0[r]

        @pl.when(pfirst[i] == 1)
        def _():
            acc_scr[...] = jnp.zeros_like(acc_scr)

        cols = lax.broadcasted_iota(jnp.int32, (sub, y_ref.shape[0]), 1)
        for k, r0 in enumerate(range(0, m.shape[0], sub)):
            @pl.when(((pbits[i] >> k) & 1) == 1)
            def _(r0=r0):
                onehot = (cols == slot[r0:r0 + sub, :]).astype(BF16)
                acc_scr[r0:r0 + sub, :] += gate[r0:r0 + sub, :] * jnp.dot(
                    onehot, y_ref[...], preferred_element_type=F32)

        @pl.when(plast[i] == 1)
        def _():
            out = x_ref[...] + acc_scr[...]
            if final_norm:
                out = _rms(out, gf_ref[...])
            o_ref[...] = out


def _pair_list(hit, npairs):
    nb = hit.shape[1]
    idx = jnp.nonzero(hit.reshape(-1), size=npairs, fill_value=-1)[0].astype(jnp.int32)
    valid = idx >= 0
    count = jnp.sum(valid.astype(jnp.int32))
    idx = jnp.where(valid, idx, idx[jnp.maximum(count - 1, 0)])
    a = idx // nb
    b = idx % nb
    off = jnp.full((1,), -1, jnp.int32)
    first = valid & (a != jnp.concatenate([off, a[:-1]]))
    nxt_valid = jnp.concatenate([valid[1:], jnp.zeros((1,), bool)])
    last = valid & ((a != jnp.concatenate([a[1:], off])) | ~nxt_valid)
    i32 = lambda v: v.astype(jnp.int32)
    return idx, a, b, i32(first), i32(last), i32(valid)


def _moe(x, g, wr, w1, w3, w2, gf, final_norm, chunk=MOE_CHUNK, tg=MOE_TILE, sub=MOE_SUB):
    n, d = x.shape
    n_exp, _, f = w1.shape
    nc = n // chunk
    spc = chunk // sub
    nt = (TOP_K * n + n_exp * (tg - 1)) // tg
    npairs = nt + n_exp * nc

    hn, meta, metat, cnt = _route(x, g, wr, chunk, sub)

    after = cnt[:, 0, :n_exp].astype(jnp.int32)
    cb = jnp.concatenate([jnp.zeros((1, n_exp), jnp.int32), after], axis=0)
    tiles_e = (cb[-1] + tg - 1) // tg
    tend = jnp.cumsum(tiles_e)
    nvalid = tend[-1]
    r = jnp.arange(nt, dtype=jnp.int32)
    rc = jnp.minimum(r, nvalid - 1)
    texp = jnp.sum((rc[:, None] >= tend[None, :]).astype(jnp.int32), axis=1)
    trank0 = (rc - (tend - tiles_e)[texp]) * tg
    lo = cb[:-1][:, texp].T
    hi = cb[1:][:, texp].T
    sub_hit = ((r < nvalid)[:, None] & (hi > lo)
               & (lo < trank0[:, None] + tg) & (hi > trank0[:, None])).reshape(nt, nc, spc)
    hit = jnp.any(sub_hit, axis=2)
    bits = jnp.sum(sub_hit.astype(jnp.int32) << jnp.arange(spc, dtype=jnp.int32), axis=2)
    _, g_tile, g_chunk, g_first, _, g_valid = _pair_list(hit, npairs)
    c_idx, c_chunk, c_tile, c_first, c_last, c_valid = _pair_list(hit.T, npairs)
    c_bits = bits.T.reshape(-1)[c_idx] * c_valid
    nvalid1 = nvalid.reshape(1).astype(jnp.int32)

    xs = pl.pallas_call(
        _gather_body,
        grid_spec=pltpu.PrefetchScalarGridSpec(
            num_scalar_prefetch=6, grid=(npairs,),
            in_specs=[pl.BlockSpec((chunk, d), lambda i, pt, pc, *_: (pc[i], 0)),
                      pl.BlockSpec((M_FIELDS, chunk), lambda i, pt, pc, *_: (0, pc[i]))],
            out_specs=pl.BlockSpec((tg, d), lambda i, pt, *_: (pt[i], 0)),
            scratch_shapes=[pltpu.VMEM((tg, d), F32)]),
        out_shape=jax.ShapeDtypeStruct((nt * tg, d), BF16),
        compiler_params=_params(("arbitrary",)),
        name="moe_gather",
    )(g_tile, g_chunk, g_first, g_valid, texp, trank0, hn, metat)

    tile = lambda i, te, nv: (jnp.minimum(i, nv[0] - 1), 0)
    wexp = lambda i, te, nv: (te[i], 0, 0)
    y = pl.pallas_call(
        functools.partial(_expert_body, fc=512),
        grid_spec=pltpu.PrefetchScalarGridSpec(
            num_scalar_prefetch=2, grid=(nt,),
            in_specs=[pl.BlockSpec((tg, d), tile), pl.BlockSpec((None, d, f), wexp),
                      pl.BlockSpec((None, d, f), wexp), pl.BlockSpec((None, f, d), wexp)],
            out_specs=pl.BlockSpec((tg, d), tile),
            scratch_shapes=[pltpu.VMEM((tg, f), BF16)]),
        out_shape=jax.ShapeDtypeStruct((nt * tg, d), BF16),
        compiler_params=_params(("arbitrary",)),
        name="moe_expert",
    )(texp, nvalid1, xs, w1, w3, w2)

    return pl.pallas_call(
        functools.partial(_combine_body, final_norm=final_norm, sub=sub),
        grid_spec=pltpu.PrefetchScalarGridSpec(
            num_scalar_prefetch=8, grid=(npairs,),
            in_specs=[pl.BlockSpec((chunk, d), lambda i, pt, pc, *_: (pc[i], 0)),
                      pl.BlockSpec((chunk, LANES), lambda i, pt, pc, *_: (pc[i], 0)),
                      pl.BlockSpec((tg, d), lambda i, pt, *_: (pt[i], 0)),
                      pl.BlockSpec((1, d), lambda i, *_: (0, 0))],
            out_specs=pl.BlockSpec((chunk, d), lambda i, pt, pc, *_: (pc[i], 0)),
            scratch_shapes=[pltpu.VMEM((chunk, d), F32)]),
        out_shape=jax.ShapeDtypeStruct((n, d), F32),
        compiler_params=_params(("arbitrary",)),
        name="moe_combine",
    )(c_tile, c_chunk, c_first, c_last, c_valid, c_bits, texp, trank0,
      x, meta, y, gf.reshape(1, d))


def _norm_body(x_ref, g_ref, o_ref):
    o_ref[...] = _rms(x_ref[...], g_ref[...])


def _final_norm(x, g, tm):
    n, d = x.shape
    row = lambda i: (i, 0)
    return pl.pallas_call(
        _norm_body,
        grid=(n // tm,),
        in_specs=[pl.BlockSpec((tm, d), row), _resident((1, d))],
        out_specs=pl.BlockSpec((tm, d), row),
        out_shape=jax.ShapeDtypeStruct((n, d), F32),
        compiler_params=_params(("parallel",)),
        name="final_norm",
    )(x, g.reshape(1, d))


def _pad_front(state, rows):
    b, r, c = state.shape
    return jnp.concatenate([jnp.zeros((b, rows - r, c), state.dtype), state], axis=1)


def kernel(x_prompt, x_sample, cache_k, cache_v, state_conv, state_pool, page_table, norm_mix, w_in, conv_w, conv_b, conv_ln_g, conv_ln_b, w_conv_out, w_attn_out, sb_bias, pool_w, pool_scale, w_o, norm_ffn, ffn_w1, ffn_w3, ffn_w2, moe_router, moe_w1, moe_w3, moe_w2, norm_final):
    bp, tp, d = x_prompt.shape
    bs, tsm, _ = x_sample.shape
    depth = w_in.shape[0]
    a = N_HEADS * HEAD_DIM
    n_pages = page_table.shape[1]
    past = n_pages * PAGE_SIZE
    cshape = (cache_k.shape[0], cache_k.shape[1], a, PAGE_SIZE)
    ck = jnp.transpose(cache_k, (0, 1, 3, 4, 2)).reshape(cshape)
    cv = jnp.transpose(cache_v, (0, 1, 3, 4, 2)).reshape(cshape)

    xp = x_prompt.reshape(bp * tp, d)
    xs = x_sample.reshape(bs * tsm, d)
    outs = {k: [] for k in ("cp", "pp", "ks", "vs", "cs", "ps")}
    kv_prompt = None
    for l in range(depth):
        w_in_l = w_in[l].astype(BF16)
        lw = (conv_w[l], conv_b[l], conv_ln_g[l], conv_ln_b[l], w_conv_out[l].astype(BF16),
              w_attn_out[l].astype(BF16), pool_w[l].astype(BF16), pool_scale[l], w_o[l].astype(BF16))

        u, q, kt_p, vt_p, p, gate = _in_proj(xp, norm_mix[l], w_in_l, tm=512,
                                             kv_seq=(l, depth, bp, tp, kv_prompt))
        kv_prompt = (kt_p, vt_p)
        c = u.shape[1]
        attn = _attn_prompt(q.reshape(bp, tp, a), kt_p, vt_p, l, sb_bias[l])
        u3 = u.reshape(bp, tp, c)
        p3 = p.reshape(bp, tp, c)
        routed = l % 2 == 1
        n_all = bp * tp + bs * tsm
        xp = _mix(xp, u3, p3, attn.reshape(bp * tp, a), gate,
                  jnp.zeros((bp, CONV_HIST, c), F32), jnp.zeros((bp, POOL_HIST, c), F32),
                  lw, nb=1, ts=256, pos0=0, into=(n_all, 0, None) if routed else None)
        outs["cp"].append(u3[:, tp - (CONV_W - 1):])
        outs["pp"].append(p3[:, tp - (POOL_MAX - 1):])

        u, q, k, v, p, gate = _in_proj(xs, norm_mix[l], w_in_l, tm=bs * tsm)
        attn = _attn_sample(q.reshape(bs, tsm, a), k.reshape(bs, tsm, a), v.reshape(bs, tsm, a),
                            ck, cv, l, page_table, sb_bias[l])
        u3 = u.reshape(bs, tsm, c)
        p3 = p.reshape(bs, tsm, c)
        xs = _mix(xs, u3, p3, attn.reshape(bs * tsm, a), gate,
                  _pad_front(state_conv[l], CONV_HIST), _pad_front(state_pool[l], POOL_HIST),
                  lw, nb=bs, ts=tsm, pos0=past, into=(n_all, bp * tp, xp) if routed else None)
        outs["ks"].append(k.reshape(bs, tsm, N_HEADS, HEAD_DIM))
        outs["vs"].append(v.reshape(bs, tsm, N_HEADS, HEAD_DIM))
        outs["cs"].append(jnp.concatenate([state_conv[l], u3], axis=1)[:, -(CONV_W - 1):])
        outs["ps"].append(jnp.concatenate([state_pool[l], p3], axis=1)[:, -(POOL_MAX - 1):])

        i = l // 2
        last = l == depth - 1
        if l % 2 == 0:
            w1, w3, w2 = ffn_w1[i].astype(BF16), ffn_w3[i].astype(BF16), ffn_w2[i].astype(BF16)
            xp = _ffn(xp, norm_ffn[l], w1, w3, w2, tm=512)
            xs = _ffn(xs, norm_ffn[l], w1, w3, w2, tm=bs * tsm)
            if last:
                xp = _final_norm(xp, norm_final, tm=512)
                xs = _final_norm(xs, norm_final, tm=bs * tsm)
        else:
            w1, w3, w2 = moe_w1[i].astype(BF16), moe_w3[i].astype(BF16), moe_w2[i].astype(BF16)
            x_all = _moe(xs, norm_ffn[l], moe_router[i], w1, w3, w2, norm_final, last)
            xp, xs = x_all[:bp * tp], x_all[bp * tp:]

    k_prompt, v_prompt = (
        jnp.transpose(t.reshape(depth, bp, N_HEADS, HEAD_DIM, tp), (0, 1, 4, 2, 3))
        for t in kv_prompt)
    return (xp.reshape(bp, tp, d), xs.reshape(bs, tsm, d),
            k_prompt, v_prompt, jnp.stack(outs["cp"]), jnp.stack(outs["pp"]),
            jnp.stack(outs["ks"]), jnp.stack(outs["vs"]), jnp.stack(outs["cs"]), jnp.stack(outs["ps"]))
```

```python
import functools

import jax
import jax.numpy as jnp
from jax import lax
from jax.experimental import pallas as pl
from jax.experimental.pallas import tpu as pltpu

F32 = jnp.float32
BF16 = jnp.bfloat16
EPS = 1e-6
LOG2E = 1.4426950408889634

CONV_W = 31
POOL_WINDOWS = (2, 4, 8, 16)
POOL_MAX = 16
N_HEADS = 8
HEAD_DIM = 64
PAGE_SIZE = 128
TOP_K = 2
CONV_HIST = 32
POOL_HIST = 16
LANES = 128
VMEM_LIMIT = 56 * 1024 * 1024


def _params(sem):
    return pltpu.CompilerParams(dimension_semantics=sem, vmem_limit_bytes=VMEM_LIMIT)


def _resident(shape):
    zeros = (0,) * len(shape)
    return pl.BlockSpec(shape, lambda *_: zeros, pipeline_mode=pl.Buffered(1))


def _rms(x, g):
    return x * lax.rsqrt(jnp.mean(x * x, axis=-1, keepdims=True) + EPS) * g


def _softplus(z):
    return jnp.maximum(z, 0.0) + jnp.log(1.0 + jnp.exp2(jnp.abs(z) * -LOG2E))


def _split_bf16(x):
    hi = x.astype(BF16)
    return hi, (x - hi.astype(F32)).astype(BF16)


def _in_proj_body(x_ref, g_ref, w_ref, *rest, q_scale, kv_t):
    if kv_t:
        stage = rest[-1]
        rest = rest[:-1]
    u_ref, q_ref, k_ref, v_ref, p_ref, gate_ref = rest[-6:]

    def put_kv(ref, val):
        if kv_t:
            stage[...] = val
            ref[...] = stage[...].T
        else:
            ref[...] = val
    hn = _rms(x_ref[...], g_ref[...]).astype(BF16)
    cd = u_ref.shape[1]
    ad = q_ref.shape[1]
    pd = p_ref.shape[1]

    def proj(c0, n):
        return jnp.dot(hn, w_ref[:, c0:c0 + n], preferred_element_type=F32)

    c = 0
    a = proj(c, cd)
    b = proj(c + cd, cd)
    u_ref[...] = a * jax.nn.sigmoid(b)
    c += 2 * cd
    q_ref[...] = (proj(c, ad) * q_scale).astype(BF16)
    c += ad
    put_kv(k_ref, proj(c, ad))
    c += ad
    put_kv(v_ref, proj(c, ad))
    c += ad
    p_ref[...] = proj(c, pd)
    c += pd
    ng = gate_ref.shape[1]
    step = 512
    for j in range(0, ng, step):
        gate_ref[:, j:j + step] = jax.nn.sigmoid(proj(c + j, step)).astype(BF16)


def _in_proj(x, g, w, tm, kv_seq=None):
    n, d = x.shape
    assert n % tm == 0
    cd = ad = pd = 512
    ng = w.shape[1] - 2 * cd - 3 * ad - pd
    row = lambda i: (i, 0)
    kv_shape, kv_spec, prev = (n, ad), pl.BlockSpec((tm, ad), row), ()
    if kv_seq is not None:
        layer, depth, batch, seq, prev = kv_seq
        assert seq % tm == 0 and batch * seq == n
        tpb = seq // tm
        kv_shape = (depth, batch, ad, seq)
        kv_spec = pl.BlockSpec((None, None, ad, tm), lambda i: (layer, i // tpb, 0, i % tpb))
        prev = () if prev is None else tuple(prev)
    outs = [
        jax.ShapeDtypeStruct((n, cd), F32),
        jax.ShapeDtypeStruct((n, ad), BF16),
        jax.ShapeDtypeStruct(kv_shape, F32),
        jax.ShapeDtypeStruct(kv_shape, F32),
        jax.ShapeDtypeStruct((n, pd), F32),
        jax.ShapeDtypeStruct((n, ng), BF16),
    ]
    out_specs = [pl.BlockSpec((tm, o.shape[-1]), row) for o in outs]
    out_specs[2] = out_specs[3] = kv_spec
    return pl.pallas_call(
        functools.partial(_in_proj_body, q_scale=HEAD_DIM ** -0.5, kv_t=kv_seq is not None),
        grid=(n // tm,),
        in_specs=([pl.BlockSpec((tm, d), row), _resident((1, d)), _resident(w.shape)]
                  + [pl.BlockSpec(memory_space=pl.ANY)] * len(prev)),
        out_specs=out_specs,
        out_shape=outs,
        input_output_aliases={3 + j: 2 + j for j in range(len(prev))},
        scratch_shapes=[pltpu.VMEM((tm, ad), F32)] if kv_seq is not None else [],
        compiler_params=_params(("parallel",)),
        name="in_proj",
    )(x, g.reshape(1, d), w, *prev)


def _attn_prompt_body(bias_ref, q_ref, kt_ref, vt_ref, o_ref, k_scr, vt_scr, acc_scr, *, tq, hp):
    t_len = q_ref.shape[0]
    nb = t_len // tq
    hd = HEAD_DIM
    g = pl.program_id(1)
    for j in range(nb):
        k_scr[j] = kt_ref[:, j * tq:(j + 1) * tq].T.astype(BF16)
        vt_scr[j] = vt_ref[:, j * tq:(j + 1) * tq].astype(BF16)
    key = lax.broadcasted_iota(jnp.int32, (tq, tq), 0)
    qry = lax.broadcasted_iota(jnp.int32, (tq, tq), 1)
    valid = key < qry
    tri = (qry >= key).astype(BF16)

    biases = [bias_ref[g * hp + hh] for hh in range(hp)]

    def block(qi, kj, cs_in, diag):
        q_rows = pl.ds(pl.multiple_of(qi * tq, tq), tq)
        heads = [slice(hh * hd, (hh + 1) * hd) for hh in range(hp)]
        zs = [lax.dot_general(k_scr[kj, :, hs], q_ref[q_rows, hs],
                              (((1,), (1,)), ((), ())), preferred_element_type=F32) + biases[hh]
              for hh, hs in enumerate(heads)]
        sps = [_softplus(z) for z in zs]
        if diag:
            sps = [jnp.where(valid, sp, 0.0) for sp in sps]
        css = [jnp.dot(tri, sp.astype(BF16), preferred_element_type=F32) for sp in sps]
        ws = [jnp.exp(z - cs - c) for z, cs, c in zip(zs, css, cs_in)]
        if diag:
            ws = [jnp.where(valid, w, 0.0) for w in ws]
        pvs = [jnp.dot(vt_scr[kj, hs, :], w.astype(BF16), preferred_element_type=F32)
               for hs, w in zip(heads, ws)]
        for hs, pv in zip(heads, pvs):
            if diag:
                acc_scr[qi, hs, :] = pv
            else:
                acc_scr[qi, hs, :] += pv
        return tuple(c + cs[0:1, :] for c, cs in zip(cs_in, css))

    def q_loop(qi, carry):
        c0 = block(qi, qi, (jnp.zeros((1, tq), F32),) * hp, True)
        lax.fori_loop(0, qi, lambda i, c: block(qi, qi - 1 - i, c, False), c0)
        return carry

    lax.fori_loop(0, nb, q_loop, 0)
    for j in range(nb):
        o_ref[j * tq:(j + 1) * tq, :] = acc_scr[j].T.astype(BF16)


def _attn_prompt(q, kt, vt, layer, bias, tq=256, hp=8):
    b, t, a = q.shape
    w = hp * HEAD_DIM
    assert t % tq == 0 and a % w == 0
    spec = pl.BlockSpec((None, t, w), lambda bi, gi: (bi, 0, gi))
    spec_t = pl.BlockSpec((None, None, w, t), lambda bi, gi: (layer, bi, gi, 0))
    nb = t // tq
    return pl.pallas_call(
        functools.partial(_attn_prompt_body, tq=tq, hp=hp),
        grid=(b, a // w),
        in_specs=[pl.BlockSpec(memory_space=pltpu.SMEM), spec, spec_t, spec_t],
        out_specs=spec,
        out_shape=jax.ShapeDtypeStruct((b, t, a), BF16),
        scratch_shapes=[pltpu.VMEM((nb, tq, w), BF16), pltpu.VMEM((nb, w, tq), BF16),
                        pltpu.VMEM((nb, w, tq), F32)],
        compiler_params=_params(("parallel", "parallel")),
        name="attn_prompt",
    )(bias, q, kt, vt)


def _attn_sample_body(pt_ref, bias_ref, q_ref, kn_ref, vn_ref, *rest, pp):
    kpg = rest[:pp]
    vpg = rest[pp:2 * pp]
    o_ref = rest[2 * pp]
    qbd_scr, bcol_scr, c_scr, acc_scr, pad_scr = rest[2 * pp + 1:]
    s = pl.program_id(1)
    t_new, a = q_ref.shape
    hd = HEAD_DIM
    nh = a // hd
    nq = nh * t_new

    def chunk(kt, vt, masked):
        r = kt.shape[1]
        sb = min(r, 2 * PAGE_SIZE)
        nblk = r // sb
        summed = lax.broadcasted_iota(jnp.int32, (sb, sb), 0)
        at = lax.broadcasted_iota(jnp.int32, (sb, sb), 1)
        tri = (summed >= at).astype(BF16)
        z = jnp.dot(qbd_scr[...], kt, preferred_element_type=F32) + bcol_scr[...]
        sp = _softplus(z)
        if masked:
            key = lax.broadcasted_iota(jnp.int32, z.shape, 1)
            qry = lax.broadcasted_iota(jnp.int32, z.shape, 0) % t_new
            ok = key < qry
            sp = jnp.where(ok, sp, 0.0)
        sp_st = jnp.concatenate([sp[:, j * sb:(j + 1) * sb] for j in range(nblk)], axis=0)
        cs_st = jnp.dot(sp_st.astype(BF16), tri, preferred_element_type=F32)
        run = c_scr[...]
        ws = [None] * nblk
        for j in reversed(range(nblk)):
            cs = cs_st[j * nq:(j + 1) * nq, :]
            ws[j] = jnp.exp(z[:, j * sb:(j + 1) * sb] - cs - run)
            run = run + cs[:, 0:1]
        c_scr[...] = run
        w = jnp.concatenate(ws, axis=1)
        if masked:
            w = jnp.where(ok, w, 0.0)
        acc_scr[...] += lax.dot_general(w.astype(BF16), vt, (((1,), (1,)), ((), ())),
                                        preferred_element_type=F32)

    @pl.when(s == 0)
    def _():
        qt = jnp.concatenate([q_ref[...].astype(F32)] * nh, axis=0)
        hq = lax.broadcasted_iota(jnp.int32, qt.shape, 0)
        col = lax.broadcasted_iota(jnp.int32, qt.shape, 1)
        qbd_scr[...] = jnp.where(hq // t_new == col // hd, qt, 0.0).astype(BF16)
        row = lax.broadcasted_iota(jnp.int32, (nq, 1), 0)
        bcol = jnp.zeros((nq, 1), F32)
        for h in range(nh):
            bcol = jnp.where(row // t_new == h, bias_ref[h], bcol)
        bcol_scr[...] = bcol
        c_scr[...] = jnp.zeros_like(c_scr)
        acc_scr[...] = jnp.zeros_like(acc_scr)
        pad_scr[...] = jnp.zeros_like(pad_scr)
        pad_scr[0:t_new, :] = kn_ref[...]
        knt = pad_scr[...].T.astype(BF16)
        pad_scr[0:t_new, :] = vn_ref[...]
        vnt = pad_scr[...].T.astype(BF16)
        chunk(knt, vnt, True)

    kt = jnp.concatenate([kpg[i][...].astype(BF16) for i in reversed(range(pp))], axis=1)
    vt = jnp.concatenate([vpg[i][...].astype(BF16) for i in reversed(range(pp))], axis=1)
    chunk(kt, vt, False)

    @pl.when(s == pl.num_programs(1) - 1)
    def _():
        col = lax.broadcasted_iota(jnp.int32, (t_new, a), 1)
        out = jnp.zeros((t_new, a), F32)
        for h in range(nh):
            out = out + jnp.where(col // hd == h, acc_scr[h * t_new:(h + 1) * t_new, :], 0.0)
        o_ref[...] = out.astype(BF16)


def _attn_sample(q, kn, vn, cache_kt, cache_vt, layer, page_table, bias, pp=32):
    b, t, a = q.shape
    n_pages = page_table.shape[1]
    assert n_pages % pp == 0 and t <= PAGE_SIZE
    nq = (a // HEAD_DIM) * t
    new_spec = pl.BlockSpec((None, t, a), lambda bi, si, pt: (bi, 0, 0))

    def page_spec(i):
        return pl.BlockSpec(
            (None, None, a, PAGE_SIZE),
            lambda bi, si, pt: (layer, pt[bi, n_pages - 1 - (si * pp + i)], 0, 0))

    grid_spec = pltpu.PrefetchScalarGridSpec(
        num_scalar_prefetch=1,
        grid=(b, n_pages // pp),
        in_specs=([pl.BlockSpec(memory_space=pltpu.SMEM), new_spec, new_spec, new_spec]
                  + [page_spec(i) for i in range(pp)] * 2),
        out_specs=new_spec,
        scratch_shapes=[
            pltpu.VMEM((nq, a), BF16),
            pltpu.VMEM((nq, 1), F32),
            pltpu.VMEM((nq, 1), F32),
            pltpu.VMEM((nq, a), F32),
            pltpu.VMEM((PAGE_SIZE, a), F32),
        ],
    )
    return pl.pallas_call(
        functools.partial(_attn_sample_body, pp=pp),
        grid_spec=grid_spec,
        out_shape=jax.ShapeDtypeStruct((b, t, a), BF16),
        compiler_params=_params(("parallel", "arbitrary")),
        name="attn_sample",
    )(page_table, bias, q, kn, vn, *([cache_kt] * pp), *([cache_vt] * pp))


def _mix_body(*refs, has_hist, has_prev, pos0):
    if has_prev:
        n_in = 16 + (2 if has_hist else 0)
        refs = refs[:n_in] + refs[n_in + 1:]
    if has_hist:
        (x_ref, u_ref, p_ref, attn_ref, gate_ref, ust_ref, pst_ref, uh_ref, ph_ref,
         cw_ref, cb_ref, lg_ref, lb_ref, wco_ref, wao_ref, pw_ref, ps_ref, wo_ref,
         o_ref, uext, pext, ycv, ush) = refs
    else:
        (x_ref, u_ref, p_ref, attn_ref, gate_ref, ust_ref, pst_ref,
         cw_ref, cb_ref, lg_ref, lb_ref, wco_ref, wao_ref, pw_ref, ps_ref, wo_ref,
         o_ref, uext, pext, ycv, ush) = refs
    i = pl.program_id(1)
    nb, ts, c = u_ref.shape
    m = nb * ts
    d = x_ref.shape[1]

    if has_hist:
        first = i == 0
        uext[:, 0:CONV_HIST, :] = jnp.where(first, ust_ref[...], uh_ref[...])
        pext[:, 0:POOL_HIST, :] = jnp.where(first, pst_ref[...], ph_ref[...])
    else:
        uext[:, 0:CONV_HIST, :] = ust_ref[...]
        pext[:, 0:POOL_HIST, :] = pst_ref[...]
    uext[:, CONV_HIST:CONV_HIST + ts, :] = u_ref[...]
    pext[:, POOL_HIST:POOL_HIST + ts, :] = p_ref[...]

    off = CONV_HIST - (CONV_W - 1)
    for b in range(1, 8):
        ush[b - 1] = uext[:, b:b + ush.shape[2], :]
    rc = min(ts, 64)
    for lg in range(c // LANES):
        ls = slice(lg * LANES, (lg + 1) * LANES)
        for r0 in range(0, ts, rc):
            acc = jnp.zeros((nb, rc, LANES), F32)
            for j in range(CONV_W):
                b = (off + j) % 8
                a = r0 + off + j - b
                win = uext[:, a:a + rc, ls] if b == 0 else ush[b - 1, :, a:a + rc, ls]
                acc = acc + cw_ref[j:j + 1, ls] * win
            ycv[:, r0:r0 + rc, ls] = acc

    yf = ycv[...].reshape(m, c) + cb_ref[...]
    mu = jnp.mean(yf, axis=-1, keepdims=True)
    yc = yf - mu
    var = jnp.mean(yc * yc, axis=-1, keepdims=True)
    yn = yc * lax.rsqrt(var + EPS) * lg_ref[...] + lb_ref[...]
    y_conv = jnp.dot((yn * jax.nn.sigmoid(yn)).astype(BF16), wco_ref[...],
                     preferred_element_type=F32)

    y_attn = jnp.dot(attn_ref[...], wao_ref[...], preferred_element_type=F32)

    pos = pos0 + i * ts + lax.broadcasted_iota(jnp.int32, (1, ts, 1), 1)
    pooled = []
    for gi, win in enumerate(POOL_WINDOWS):
        ls = slice(gi * LANES, (gi + 1) * LANES)
        cur = pext[:, POOL_HIST:POOL_HIST + ts, ls]
        ws = cur
        for k in range(1, win):
            ws = ws + pext[:, POOL_HIST - k:POOL_HIST - k + ts, ls]
        cnt = jnp.minimum(pos + 1, win).astype(F32)
        pg = (ws / cnt - cur).reshape(m, LANES).astype(BF16)
        pooled.append(jnp.dot(pg, pw_ref[gi], preferred_element_type=F32))
    y_pool = jnp.concatenate(pooled, axis=-1) * ps_ref[...]

    g0 = gate_ref[:, 0:d].astype(F32)
    g1 = gate_ref[:, d:2 * d].astype(F32)
    g2 = gate_ref[:, 2 * d:3 * d].astype(F32)
    merged = g0 * y_conv + g1 * y_attn + g2 * y_pool
    o_ref[...] = x_ref[...] + jnp.dot(merged.astype(BF16), wo_ref[...], preferred_element_type=F32)


def _mix(x, u, p, attn, gate, ust, pst, lw, nb, ts, pos0, into=None):
    (conv_w, conv_b, ln_g, ln_b, wco, wao, pool_w, pool_scale, wo) = lw
    b, t, c = u.shape
    d = x.shape[1]
    nt = t // ts
    has_hist = nt > 1
    assert t % ts == 0 and b % nb == 0 and (nb == 1 or not has_hist)
    out_rows, row0, prev = (x.shape[0], 0, None) if into is None else into
    assert row0 % (nb * ts) == 0 and out_rows % (nb * ts) == 0
    blk0 = row0 // (nb * ts)
    tok = lambda bi, ti: (bi * nt + ti, 0)
    tok_out = lambda bi, ti: (bi * nt + ti + blk0, 0)
    seq = lambda bi, ti: (bi, ti, 0)
    in_specs = [
        pl.BlockSpec((nb * ts, d), tok),
        pl.BlockSpec((nb, ts, c), seq),
        pl.BlockSpec((nb, ts, c), seq),
        pl.BlockSpec((nb * ts, c), tok),
        pl.BlockSpec((nb * ts, 3 * d), tok),
        pl.BlockSpec((nb, CONV_HIST, c), lambda bi, ti: (bi, 0, 0)),
        pl.BlockSpec((nb, POOL_HIST, c), lambda bi, ti: (bi, 0, 0)),
    ]
    args = [x, u, p, attn, gate, ust, pst]
    if has_hist:
        in_specs += [
            pl.BlockSpec((nb, CONV_HIST, c),
                         lambda bi, ti: (bi, jnp.maximum(ti * (ts // CONV_HIST) - 1, 0), 0)),
            pl.BlockSpec((nb, POOL_HIST, c),
                         lambda bi, ti: (bi, jnp.maximum(ti * (ts // POOL_HIST) - 1, 0), 0)),
        ]
        args += [u, p]
    consts = [conv_w, conv_b.reshape(1, c), ln_g.reshape(1, c), ln_b.reshape(1, c),
              wco, wao, pool_w, pool_scale.reshape(1, d), wo]
    in_specs += [_resident(a.shape) for a in consts]
    args += consts
    aliases = {}
    if prev is not None:
        aliases = {len(args): 0}
        in_specs.append(pl.BlockSpec(memory_space=pl.ANY))
        args.append(prev)
    return pl.pallas_call(
        functools.partial(_mix_body, has_hist=has_hist, has_prev=prev is not None, pos0=pos0),
        grid=(b // nb, nt),
        in_specs=in_specs,
        out_specs=pl.BlockSpec((nb * ts, d), tok_out),
        out_shape=jax.ShapeDtypeStruct((out_rows, d), F32),
        input_output_aliases=aliases,
        scratch_shapes=[pltpu.VMEM((nb, CONV_HIST + ts, c), F32),
                        pltpu.VMEM((nb, POOL_HIST + ts, c), F32),
                        pltpu.VMEM((nb, ts, c), F32),
                        pltpu.VMEM((7, nb, CONV_HIST + ts - 8, c), F32)],
        compiler_params=_params(("parallel", "parallel")),
        name="mix",
    )(*args)


def _swiglu_into(hn, w1_ref, w3_ref, a_scr, fc):
    f = a_scr.shape[1]
    for c0 in range(0, f, fc):
        n = min(fc, f - c0)
        h1 = jnp.dot(hn, w1_ref[:, c0:c0 + n], preferred_element_type=F32)
        h3 = jnp.dot(hn, w3_ref[:, c0:c0 + n], preferred_element_type=F32)
        a_scr[:, c0:c0 + n] = (h1 * jax.nn.sigmoid(h1) * h3).astype(BF16)


def _ffn_body(x_ref, g_ref, w1_ref, w3_ref, w2_ref, o_ref, a_scr, *, fc):
    x = x_ref[...]
    hn = _rms(x, g_ref[...]).astype(BF16)
    _swiglu_into(hn, w1_ref, w3_ref, a_scr, fc)
    o_ref[...] = x + jnp.dot(a_scr[...], w2_ref[...], preferred_element_type=F32)


def _ffn(x, g, w1, w3, w2, tm):
    n, d = x.shape
    assert n % tm == 0
    f = w1.shape[1]
    row = lambda i: (i, 0)
    return pl.pallas_call(
        functools.partial(_ffn_body, fc=512),
        grid=(n // tm,),
        in_specs=[pl.BlockSpec((tm, d), row), _resident((1, d)),
                  _resident(w1.shape), _resident(w3.shape), _resident(w2.shape)],
        out_specs=pl.BlockSpec((tm, d), row),
        out_shape=jax.ShapeDtypeStruct((n, d), F32),
        scratch_shapes=[pltpu.VMEM((tm, f), BF16)],
        compiler_params=_params(("parallel",)),
        name="ffn",
    )(x, g.reshape(1, d), w1, w3, w2)


MOE_CHUNK = 1280
MOE_TILE = 256
MOE_SUB = 256
M_E1, M_E2, M_POS1, M_POS2, M_P1, M_P2 = range(6)
M_FIELDS = 8


def _route_top2(hn_f32, wr_ref, n_exp):
    xh, xl = _split_bf16(hn_f32)
    wh, wl = _split_bf16(wr_ref[...])
    logits = (jnp.dot(xh, wh, preferred_element_type=F32)
              + jnp.dot(xh, wl, preferred_element_type=F32)
              + jnp.dot(xl, wh, preferred_element_type=F32))
    lane = lax.broadcasted_iota(jnp.int32, logits.shape, 1)
    neg = jnp.float32(-jnp.inf)
    l1 = jnp.where(lane < n_exp, logits, neg)
    m1 = jnp.max(l1, axis=-1, keepdims=True)
    i1 = jnp.min(jnp.where(l1 == m1, lane, LANES), axis=-1, keepdims=True)
    l2 = jnp.where(lane == i1, neg, l1)
    m2 = jnp.max(l2, axis=-1, keepdims=True)
    i2 = jnp.min(jnp.where(l2 == m2, lane, LANES), axis=-1, keepdims=True)
    e2 = jnp.exp(m2 - m1)
    return i1, i2, 1.0 / (1.0 + e2), e2 / (1.0 + e2)


def _route_body(x_ref, g_ref, wr_ref, hn_ref, meta_ref, metat_ref, cnt_ref, run_scr, *, n_exp, sub):
    @pl.when(pl.program_id(0) == 0)
    def _():
        run_scr[...] = jnp.zeros_like(run_scr)

    hn = _rms(x_ref[...], g_ref[...])
    hn_ref[...] = hn.astype(BF16)
    i1, i2, p1, p2 = _route_top2(hn, wr_ref, n_exp)
    m = hn.shape[0]
    lane = lax.broadcasted_iota(jnp.int32, (m, LANES), 1)
    sel1 = lane == i1
    sel2 = lane == i2
    assign = (sel1 | sel2).astype(BF16)
    row = lax.broadcasted_iota(jnp.int32, (sub, sub), 0)
    col = lax.broadcasted_iota(jnp.int32, (sub, sub), 1)
    before = (col < row).astype(BF16)
    run = run_scr[0:1, :]
    ranks = []
    for k, r0 in enumerate(range(0, m, sub)):
        a = assign[r0:r0 + sub, :]
        ranks.append(jnp.dot(before, a, preferred_element_type=F32) + run)
        run = run + jnp.sum(a.astype(F32), axis=0, keepdims=True)
        cnt_ref[k] = jnp.broadcast_to(run, cnt_ref.shape[1:])
    rank = jnp.concatenate(ranks, axis=0)
    run_scr[...] = jnp.broadcast_to(run, run_scr.shape)
    pos1 = jnp.sum(jnp.where(sel1, rank, 0.0), axis=-1, keepdims=True)
    pos2 = jnp.sum(jnp.where(sel2, rank, 0.0), axis=-1, keepdims=True)
    fields = {M_E1: i1.astype(F32), M_E2: i2.astype(F32), M_POS1: pos1, M_POS2: pos2,
              M_P1: p1, M_P2: p2}
    meta = jnp.zeros((m, LANES), F32)
    for j, val in fields.items():
        meta = jnp.where(lane == j, val, meta)
    meta_ref[...] = meta
    metat_ref[...] = meta.T[0:M_FIELDS, :]


def _route(x, g, wr, chunk, sub):
    n, d = x.shape
    n_exp = wr.shape[1]
    wr_pad = jnp.zeros((d, LANES), F32).at[:, :n_exp].set(wr)
    assert n % chunk == 0 and chunk % sub == 0
    nc = n // chunk
    spc = chunk // sub
    row = lambda c: (c, 0)
    return pl.pallas_call(
        functools.partial(_route_body, n_exp=n_exp, sub=sub),
        grid=(nc,),
        in_specs=[pl.BlockSpec((chunk, d), row), _resident((1, d)), _resident((d, LANES))],
        out_specs=[pl.BlockSpec((chunk, d), row), pl.BlockSpec((chunk, LANES), row),
                   pl.BlockSpec((M_FIELDS, chunk), lambda c: (0, c)),
                   pl.BlockSpec((spc, 8, LANES), lambda c: (c, 0, 0))],
        out_shape=[jax.ShapeDtypeStruct((n, d), BF16),
                   jax.ShapeDtypeStruct((n, LANES), F32),
                   jax.ShapeDtypeStruct((M_FIELDS, n), F32),
                   jax.ShapeDtypeStruct((n // sub, 8, LANES), F32)],
        scratch_shapes=[pltpu.VMEM((8, LANES), F32)],
        compiler_params=_params(("arbitrary",)),
        name="moe_route",
    )(x, g.reshape(1, d), wr_pad)


def _expert_body(ptile, pchunk, pfirst, plast, pvalid, texp, trank0,
                 hn_ref, metat_ref, w1_ref, w3_ref, w2_ref, y_ref, acc_scr, a_scr, *, fc):
    i = pl.program_id(0)

    @pl.when(pvalid[i] == 1)
    def _():
        r = ptile[i]
        e = texp[r]
        mt = metat_ref[...]
        e1 = mt[M_E1:M_E1 + 1, :].astype(jnp.int32)
        e2 = mt[M_E2:M_E2 + 1, :].astype(jnp.int32)
        pos = jnp.where(e1 == e, mt[M_POS1:M_POS1 + 1, :],
                        jnp.where(e2 == e, mt[M_POS2:M_POS2 + 1, :], -1.0))
        slot = pos.astype(jnp.int32) - trank0[r]
        rows = lax.broadcasted_iota(jnp.int32, (acc_scr.shape[0], slot.shape[1]), 0)
        onehot = (rows == slot).astype(BF16)
        res = jnp.dot(onehot, hn_ref[...], preferred_element_type=F32)

        @pl.when(pfirst[i] == 1)
        def _():
            acc_scr[...] = res

        @pl.when(pfirst[i] == 0)
        def _():
            acc_scr[...] += res

        @pl.when(plast[i] == 1)
        def _():
            _swiglu_into(acc_scr[...].astype(BF16), w1_ref, w3_ref, a_scr, fc)
            y_ref[...] = jnp.dot(a_scr[...], w2_ref[...],
                                 preferred_element_type=F32).astype(BF16)


def _combine_body(ptile, pchunk, pfirst, plast, pvalid, pbits, texp, trank0,
                  x_ref, meta_ref, y_ref, gf_ref, o_ref, acc_scr, *, final_norm, sub):
    i = pl.program_id(0)

    @pl.when(pvalid[i] == 1)
    def _():
        r = ptile[i]
        e = texp[r]
        m = meta_ref[...]
        is1 = m[:, M_E1:M_E1 + 1].astype(jnp.int32) == e
        is2 = m[:, M_E2:M_E2 + 1].astype(jnp.int32) == e
        pos = jnp.where(is1, m[:, M_POS1:M_POS1 + 1],
                        jnp.where(is2, m[:, M_POS2:M_POS2 + 1], -1.0))
        gate = jnp.where(is1, m[:, M_P1:M_P1 + 1], jnp.where(is2, m[:, M_P2:M_P2 + 1], 0.0))
        slot = pos.astype(jnp.int32) - trank0[r]

        @pl.when(pfirst[i] == 1)
        def _():
            acc_scr[...] = jnp.zeros_like(acc_scr)

        cols = lax.broadcasted_iota(jnp.int32, (sub, y_ref.shape[0]), 1)
        for k, r0 in enumerate(range(0, m.shape[0], sub)):
            @pl.when(((pbits[i] >> k) & 1) == 1)
            def _(r0=r0):
                onehot = (cols == slot[r0:r0 + sub, :]).astype(BF16)
                acc_scr[r0:r0 + sub, :] += gate[r0:r0 + sub, :] * jnp.dot(
                    onehot, y_ref[...], preferred_element_type=F32)

        @pl.when(plast[i] == 1)
        def _():
            out = x_ref[...] + acc_scr[...]
            if final_norm:
                out = _rms(out, gf_ref[...])
            o_ref[...] = out


def _pair_list(hit, npairs):
    nb = hit.shape[1]
    idx = jnp.nonzero(hit.reshape(-1), size=npairs, fill_value=-1)[0].astype(jnp.int32)
    valid = idx >= 0
    count = jnp.sum(valid.astype(jnp.int32))
    idx = jnp.where(valid, idx, idx[jnp.maximum(count - 1, 0)])
    a = idx // nb
    b = idx % nb
    off = jnp.full((1,), -1, jnp.int32)
    first = valid & (a != jnp.concatenate([off, a[:-1]]))
    nxt_valid = jnp.concatenate([valid[1:], jnp.zeros((1,), bool)])
    last = valid & ((a != jnp.concatenate([a[1:], off])) | ~nxt_valid)
    i32 = lambda v: v.astype(jnp.int32)
    return idx, a, b, i32(first), i32(last), i32(valid)


def _moe(x, g, wr, w1, w3, w2, gf, final_norm, chunk=MOE_CHUNK, tg=MOE_TILE, sub=MOE_SUB):
    n, d = x.shape
    n_exp, _, f = w1.shape
    nc = n // chunk
    spc = chunk // sub
    nt = (TOP_K * n + n_exp * (tg - 1)) // tg
    npairs = nt + n_exp * nc

    hn, meta, metat, cnt = _route(x, g, wr, chunk, sub)

    after = cnt[:, 0, :n_exp].astype(jnp.int32)
    cb = jnp.concatenate([jnp.zeros((1, n_exp), jnp.int32), after], axis=0)
    tiles_e = (cb[-1] + tg - 1) // tg
    tend = jnp.cumsum(tiles_e)
    nvalid = tend[-1]
    r = jnp.arange(nt, dtype=jnp.int32)
    rc = jnp.minimum(r, nvalid - 1)
    texp = jnp.sum((rc[:, None] >= tend[None, :]).astype(jnp.int32), axis=1)
    trank0 = (rc - (tend - tiles_e)[texp]) * tg
    lo = cb[:-1][:, texp].T
    hi = cb[1:][:, texp].T
    sub_hit = ((r < nvalid)[:, None] & (hi > lo)
               & (lo < trank0[:, None] + tg) & (hi > trank0[:, None])).reshape(nt, nc, spc)
    hit = jnp.any(sub_hit, axis=2)
    bits = jnp.sum(sub_hit.astype(jnp.int32) << jnp.arange(spc, dtype=jnp.int32), axis=2)
    _, g_tile, g_chunk, g_first, g_last, g_valid = _pair_list(hit, npairs)
    c_idx, c_chunk, c_tile, c_first, c_last, c_valid = _pair_list(hit.T, npairs)
    c_bits = bits.T.reshape(-1)[c_idx] * c_valid

    wexp = lambda i, pt, pc, pf, pl_, pv, te, tr: (te[pt[i]], 0, 0)
    y = pl.pallas_call(
        functools.partial(_expert_body, fc=512),
        grid_spec=pltpu.PrefetchScalarGridSpec(
            num_scalar_prefetch=7, grid=(npairs,),
            in_specs=[pl.BlockSpec((chunk, d), lambda i, pt, pc, *_: (pc[i], 0)),
                      pl.BlockSpec((M_FIELDS, chunk), lambda i, pt, pc, *_: (0, pc[i])),
                      pl.BlockSpec((None, d, f), wexp), pl.BlockSpec((None, d, f), wexp),
                      pl.BlockSpec((None, f, d), wexp)],
            out_specs=pl.BlockSpec((tg, d), lambda i, pt, *_: (pt[i], 0)),
            scratch_shapes=[pltpu.VMEM((tg, d), F32), pltpu.VMEM((tg, f), BF16)]),
        out_shape=jax.ShapeDtypeStruct((nt * tg, d), BF16),
        compiler_params=_params(("arbitrary",)),
        name="moe_expert",
    )(g_tile, g_chunk, g_first, g_last, g_valid, texp, trank0, hn, metat, w1, w3, w2)

    return pl.pallas_call(
        functools.partial(_combine_body, final_norm=final_norm, sub=sub),
        grid_spec=pltpu.PrefetchScalarGridSpec(
            num_scalar_prefetch=8, grid=(npairs,),
            in_specs=[pl.BlockSpec((chunk, d), lambda i, pt, pc, *_: (pc[i], 0)),
                      pl.BlockSpec((chunk, LANES), lambda i, pt, pc, *_: (pc[i], 0)),
                      pl.BlockSpec((tg, d), lambda i, pt, *_: (pt[i], 0)),
                      pl.BlockSpec((1, d), lambda i, *_: (0, 0))],
            out_specs=pl.BlockSpec((chunk, d), lambda i, pt, pc, *_: (pc[i], 0)),
            scratch_shapes=[pltpu.VMEM((chunk, d), F32)]),
        out_shape=jax.ShapeDtypeStruct((n, d), F32),
        compiler_params=_params(("arbitrary",)),
        name="moe_combine",
    )(c_tile, c_chunk, c_first, c_last, c_valid, c_bits, texp, trank0,
      x, meta, y, gf.reshape(1, d))


def _norm_body(x_ref, g_ref, o_ref):
    o_ref[...] = _rms(x_ref[...], g_ref[...])


def _final_norm(x, g, tm):
    n, d = x.shape
    row = lambda i: (i, 0)
    return pl.pallas_call(
        _norm_body,
        grid=(n // tm,),
        in_specs=[pl.BlockSpec((tm, d), row), _resident((1, d))],
        out_specs=pl.BlockSpec((tm, d), row),
        out_shape=jax.ShapeDtypeStruct((n, d), F32),
        compiler_params=_params(("parallel",)),
        name="final_norm",
    )(x, g.reshape(1, d))


def _pad_front(state, rows):
    b, r, c = state.shape
    return jnp.concatenate([jnp.zeros((b, rows - r, c), state.dtype), state], axis=1)


def kernel(x_prompt, x_sample, cache_k, cache_v, state_conv, state_pool, page_table, norm_mix, w_in, conv_w, conv_b, conv_ln_g, conv_ln_b, w_conv_out, w_attn_out, sb_bias, pool_w, pool_scale, w_o, norm_ffn, ffn_w1, ffn_w3, ffn_w2, moe_router, moe_w1, moe_w3, moe_w2, norm_final):
    bp, tp, d = x_prompt.shape
    bs, tsm, _ = x_sample.shape
    depth = w_in.shape[0]
    a = N_HEADS * HEAD_DIM
    n_pages = page_table.shape[1]
    past = n_pages * PAGE_SIZE
    cshape = (cache_k.shape[0], cache_k.shape[1], a, PAGE_SIZE)
    ck = jnp.transpose(cache_k, (0, 1, 3, 4, 2)).reshape(cshape)
    cv = jnp.transpose(cache_v, (0, 1, 3, 4, 2)).reshape(cshape)

    xp = x_prompt.reshape(bp * tp, d)
    xs = x_sample.reshape(bs * tsm, d)
    outs = {k: [] for k in ("cp", "pp", "ks", "vs", "cs", "ps")}
    kv_prompt = None
    for l in range(depth):
        w_in_l = w_in[l].astype(BF16)
        lw = (conv_w[l], conv_b[l], conv_ln_g[l], conv_ln_b[l], w_conv_out[l].astype(BF16),
              w_attn_out[l].astype(BF16), pool_w[l].astype(BF16), pool_scale[l], w_o[l].astype(BF16))

        u, q, kt_p, vt_p, p, gate = _in_proj(xp, norm_mix[l], w_in_l, tm=512,
                                             kv_seq=(l, depth, bp, tp, kv_prompt))
        kv_prompt = (kt_p, vt_p)
        c = u.shape[1]
        attn = _attn_prompt(q.reshape(bp, tp, a), kt_p, vt_p, l, sb_bias[l])
        u3 = u.reshape(bp, tp, c)
        p3 = p.reshape(bp, tp, c)
        routed = l % 2 == 1
        n_all = bp * tp + bs * tsm
        xp = _mix(xp, u3, p3, attn.reshape(bp * tp, a), gate,
                  jnp.zeros((bp, CONV_HIST, c), F32), jnp.zeros((bp, POOL_HIST, c), F32),
                  lw, nb=1, ts=256, pos0=0, into=(n_all, 0, None) if routed else None)
        outs["cp"].append(u3[:, tp - (CONV_W - 1):])
        outs["pp"].append(p3[:, tp - (POOL_MAX - 1):])

        u, q, k, v, p, gate = _in_proj(xs, norm_mix[l], w_in_l, tm=bs * tsm)
        attn = _attn_sample(q.reshape(bs, tsm, a), k.reshape(bs, tsm, a), v.reshape(bs, tsm, a),
                            ck, cv, l, page_table, sb_bias[l])
        u3 = u.reshape(bs, tsm, c)
        p3 = p.reshape(bs, tsm, c)
        xs = _mix(xs, u3, p3, attn.reshape(bs * tsm, a), gate,
                  _pad_front(state_conv[l], CONV_HIST), _pad_front(state_pool[l], POOL_HIST),
                  lw, nb=bs, ts=tsm, pos0=past, into=(n_all, bp * tp, xp) if routed else None)
        outs["ks"].append(k.reshape(bs, tsm, N_HEADS, HEAD_DIM))
        outs["vs"].append(v.reshape(bs, tsm, N_HEADS, HEAD_DIM))
        outs["cs"].append(jnp.concatenate([state_conv[l], u3], axis=1)[:, -(CONV_W - 1):])
        outs["ps"].append(jnp.concatenate([state_pool[l], p3], axis=1)[:, -(POOL_MAX - 1):])

        i = l // 2
        last = l == depth - 1
        if l % 2 == 0:
            w1, w3, w2 = ffn_w1[i].astype(BF16), ffn_w3[i].astype(BF16), ffn_w2[i].astype(BF16)
            xp = _ffn(xp, norm_ffn[l], w1, w3, w2, tm=512)
            xs = _ffn(xs, norm_ffn[l], w1, w3, w2, tm=bs * tsm)
            if last:
                xp = _final_norm(xp, norm_final, tm=512)
                xs = _final_norm(xs, norm_final, tm=bs * tsm)
        else:
            w1, w3, w2 = moe_w1[i].astype(BF16), moe_w3[i].astype(BF16), moe_w2[i].astype(BF16)
            x_all = _moe(xs, norm_ffn[l], moe_router[i], w1, w3, w2, norm_final, last)
            xp, xs = x_all[:bp * tp], x_all[bp * tp:]

    k_prompt, v_prompt = (
        jnp.transpose(t.reshape(depth, bp, N_HEADS, HEAD_DIM, tp), (0, 1, 4, 2, 3))
        for t in kv_prompt)
    return (xp.reshape(bp, tp, d), xs.reshape(bs, tsm, d),
            k_prompt, v_prompt, jnp.stack(outs["cp"]), jnp.stack(outs["pp"]),
            jnp.stack(outs["ks"]), jnp.stack(outs["vs"]), jnp.stack(outs["cs"]), jnp.stack(outs["ps"]))
```

```python
import functools

import jax
import jax.numpy as jnp
from jax import lax
from jax.experimental import pallas as pl
from jax.experimental.pallas import tpu as pltpu

F32 = jnp.float32
BF16 = jnp.bfloat16
EPS = 1e-6
LOG2E = 1.4426950408889634

CONV_W = 31
POOL_WINDOWS = (2, 4, 8, 16)
POOL_MAX = 16
N_HEADS = 8
HEAD_DIM = 64
PAGE_SIZE = 128
TOP_K = 2
CONV_HIST = 32
POOL_HIST = 16
LANES = 128
VMEM_LIMIT = 56 * 1024 * 1024


def _params(sem):
    return pltpu.CompilerParams(dimension_semantics=sem, vmem_limit_bytes=VMEM_LIMIT)


def _resident(shape):
    zeros = (0,) * len(shape)
    return pl.BlockSpec(shape, lambda *_: zeros, pipeline_mode=pl.Buffered(1))


def _rms(x, g):
    return x * lax.rsqrt(jnp.mean(x * x, axis=-1, keepdims=True) + EPS) * g


def _softplus(z):
    return jnp.maximum(z, 0.0) + jnp.log(1.0 + jnp.exp2(jnp.abs(z) * -LOG2E))


def _split_bf16(x):
    hi = x.astype(BF16)
    return hi, (x - hi.astype(F32)).astype(BF16)


def _in_proj_body(x_ref, g_ref, w_ref, *rest, q_scale, kv_t):
    if kv_t:
        stage = rest[-1]
        rest = rest[:-1]
    u_ref, q_ref, k_ref, v_ref, p_ref, gate_ref = rest[-6:]

    def put_kv(ref, val):
        if kv_t:
            stage[...] = val
            ref[...] = stage[...].T
        else:
            ref[...] = val
    hn = _rms(x_ref[...], g_ref[...]).astype(BF16)
    cd = u_ref.shape[1]
    ad = q_ref.shape[1]
    pd = p_ref.shape[1]

    def proj(c0, n):
        return jnp.dot(hn, w_ref[:, c0:c0 + n], preferred_element_type=F32)

    c = 0
    a = proj(c, cd)
    b = proj(c + cd, cd)
    u_ref[...] = a * jax.nn.sigmoid(b)
    c += 2 * cd
    q_ref[...] = (proj(c, ad) * q_scale).astype(BF16)
    c += ad
    put_kv(k_ref, proj(c, ad))
    c += ad
    put_kv(v_ref, proj(c, ad))
    c += ad
    p_ref[...] = proj(c, pd)
    c += pd
    ng = gate_ref.shape[1]
    step = 512
    for j in range(0, ng, step):
        gate_ref[:, j:j + step] = jax.nn.sigmoid(proj(c + j, step)).astype(BF16)


def _in_proj(x, g, w, tm, kv_seq=None):
    n, d = x.shape
    assert n % tm == 0
    cd = ad = pd = 512
    ng = w.shape[1] - 2 * cd - 3 * ad - pd
    row = lambda i: (i, 0)
    kv_shape, kv_spec, prev = (n, ad), pl.BlockSpec((tm, ad), row), ()
    if kv_seq is not None:
        layer, depth, batch, seq, prev = kv_seq
        assert seq % tm == 0 and batch * seq == n
        tpb = seq // tm
        kv_shape = (depth, batch, ad, seq)
        kv_spec = pl.BlockSpec((None, None, ad, tm), lambda i: (layer, i // tpb, 0, i % tpb))
        prev = () if prev is None else tuple(prev)
    outs = [
        jax.ShapeDtypeStruct((n, cd), F32),
        jax.ShapeDtypeStruct((n, ad), BF16),
        jax.ShapeDtypeStruct(kv_shape, F32),
        jax.ShapeDtypeStruct(kv_shape, F32),
        jax.ShapeDtypeStruct((n, pd), F32),
        jax.ShapeDtypeStruct((n, ng), BF16),
    ]
    out_specs = [pl.BlockSpec((tm, o.shape[-1]), row) for o in outs]
    out_specs[2] = out_specs[3] = kv_spec
    return pl.pallas_call(
        functools.partial(_in_proj_body, q_scale=HEAD_DIM ** -0.5, kv_t=kv_seq is not None),
        grid=(n // tm,),
        in_specs=([pl.BlockSpec((tm, d), row), _resident((1, d)), _resident(w.shape)]
                  + [pl.BlockSpec(memory_space=pl.ANY)] * len(prev)),
        out_specs=out_specs,
        out_shape=outs,
        input_output_aliases={3 + j: 2 + j for j in range(len(prev))},
        scratch_shapes=[pltpu.VMEM((tm, ad), F32)] if kv_seq is not None else [],
        compiler_params=_params(("parallel",)),
        name="in_proj",
    )(x, g.reshape(1, d), w, *prev)


def _attn_prompt_body(bias_ref, q_ref, kt_ref, vt_ref, o_ref, k_scr, vt_scr, acc_scr, *, tq, hp):
    t_len = q_ref.shape[0]
    nb = t_len // tq
    hd = HEAD_DIM
    g = pl.program_id(1)
    for j in range(nb):
        k_scr[j] = kt_ref[:, j * tq:(j + 1) * tq].T.astype(BF16)
        vt_scr[j] = vt_ref[:, j * tq:(j + 1) * tq].astype(BF16)
    key = lax.broadcasted_iota(jnp.int32, (tq, tq), 0)
    qry = lax.broadcasted_iota(jnp.int32, (tq, tq), 1)
    valid = key < qry
    tri = (qry >= key).astype(BF16)

    biases = [bias_ref[g * hp + hh] for hh in range(hp)]

    def block(qi, kj, cs_in, diag):
        q_rows = pl.ds(pl.multiple_of(qi * tq, tq), tq)
        heads = [slice(hh * hd, (hh + 1) * hd) for hh in range(hp)]
        zs = [lax.dot_general(k_scr[kj, :, hs], q_ref[q_rows, hs],
                              (((1,), (1,)), ((), ())), preferred_element_type=F32) + biases[hh]
              for hh, hs in enumerate(heads)]
        sps = [_softplus(z) for z in zs]
        if diag:
            sps = [jnp.where(valid, sp, 0.0) for sp in sps]
        css = [jnp.dot(tri, sp.astype(BF16), preferred_element_type=F32) for sp in sps]
        ws = [jnp.exp(z - cs - c) for z, cs, c in zip(zs, css, cs_in)]
        if diag:
            ws = [jnp.where(valid, w, 0.0) for w in ws]
        pvs = [jnp.dot(vt_scr[kj, hs, :], w.astype(BF16), preferred_element_type=F32)
               for hs, w in zip(heads, ws)]
        for hs, pv in zip(heads, pvs):
            if diag:
                acc_scr[qi, hs, :] = pv
            else:
                acc_scr[qi, hs, :] += pv
        return tuple(c + cs[0:1, :] for c, cs in zip(cs_in, css))

    def q_loop(qi, carry):
        c0 = block(qi, qi, (jnp.zeros((1, tq), F32),) * hp, True)
        lax.fori_loop(0, qi, lambda i, c: block(qi, qi - 1 - i, c, False), c0)
        return carry

    lax.fori_loop(0, nb, q_loop, 0)
    for j in range(nb):
        o_ref[j * tq:(j + 1) * tq, :] = acc_scr[j].T.astype(BF16)


def _attn_prompt(q, kt, vt, layer, bias, tq=256, hp=8):
    b, t, a = q.shape
    w = hp * HEAD_DIM
    assert t % tq == 0 and a % w == 0
    spec = pl.BlockSpec((None, t, w), lambda bi, gi: (bi, 0, gi))
    spec_t = pl.BlockSpec((None, None, w, t), lambda bi, gi: (layer, bi, gi, 0))
    nb = t // tq
    return pl.pallas_call(
        functools.partial(_attn_prompt_body, tq=tq, hp=hp),
        grid=(b, a // w),
        in_specs=[pl.BlockSpec(memory_space=pltpu.SMEM), spec, spec_t, spec_t],
        out_specs=spec,
        out_shape=jax.ShapeDtypeStruct((b, t, a), BF16),
        scratch_shapes=[pltpu.VMEM((nb, tq, w), BF16), pltpu.VMEM((nb, w, tq), BF16),
                        pltpu.VMEM((nb, w, tq), F32)],
        compiler_params=_params(("parallel", "parallel")),
        name="attn_prompt",
    )(bias, q, kt, vt)


def _attn_sample_body(pt_ref, bias_ref, q_ref, kn_ref, vn_ref, *rest, pp):
    kpg = rest[:pp]
    vpg = rest[pp:2 * pp]
    o_ref = rest[2 * pp]
    qbd_scr, bcol_scr, c_scr, acc_scr, pad_scr = rest[2 * pp + 1:]
    s = pl.program_id(1)
    t_new, a = q_ref.shape
    hd = HEAD_DIM
    nh = a // hd
    nq = nh * t_new

    def chunk(kt, vt, masked):
        r = kt.shape[1]
        sb = min(r, 2 * PAGE_SIZE)
        nblk = r // sb
        summed = lax.broadcasted_iota(jnp.int32, (sb, sb), 0)
        at = lax.broadcasted_iota(jnp.int32, (sb, sb), 1)
        tri = (summed >= at).astype(BF16)
        z = jnp.dot(qbd_scr[...], kt, preferred_element_type=F32) + bcol_scr[...]
        sp = _softplus(z)
        if masked:
            key = lax.broadcasted_iota(jnp.int32, z.shape, 1)
            qry = lax.broadcasted_iota(jnp.int32, z.shape, 0) % t_new
            ok = key < qry
            sp = jnp.where(ok, sp, 0.0)
        sp_st = jnp.concatenate([sp[:, j * sb:(j + 1) * sb] for j in range(nblk)], axis=0)
        cs_st = jnp.dot(sp_st.astype(BF16), tri, preferred_element_type=F32)
        run = c_scr[...]
        ws = [None] * nblk
        for j in reversed(range(nblk)):
            cs = cs_st[j * nq:(j + 1) * nq, :]
            ws[j] = jnp.exp(z[:, j * sb:(j + 1) * sb] - cs - run)
            run = run + cs[:, 0:1]
        c_scr[...] = run
        w = jnp.concatenate(ws, axis=1)
        if masked:
            w = jnp.where(ok, w, 0.0)
        acc_scr[...] += lax.dot_general(w.astype(BF16), vt, (((1,), (1,)), ((), ())),
                                        preferred_element_type=F32)

    @pl.when(s == 0)
    def _():
        qt = jnp.concatenate([q_ref[...].astype(F32)] * nh, axis=0)
        hq = lax.broadcasted_iota(jnp.int32, qt.shape, 0)
        col = lax.broadcasted_iota(jnp.int32, qt.shape, 1)
        qbd_scr[...] = jnp.where(hq // t_new == col // hd, qt, 0.0).astype(BF16)
        row = lax.broadcasted_iota(jnp.int32, (nq, 1), 0)
        bcol = jnp.zeros((nq, 1), F32)
        for h in range(nh):
            bcol = jnp.where(row // t_new == h, bias_ref[h], bcol)
        bcol_scr[...] = bcol
        c_scr[...] = jnp.zeros_like(c_scr)
        acc_scr[...] = jnp.zeros_like(acc_scr)
        pad_scr[...] = jnp.zeros_like(pad_scr)
        pad_scr[0:t_new, :] = kn_ref[...]
        knt = pad_scr[...].T.astype(BF16)
        pad_scr[0:t_new, :] = vn_ref[...]
        vnt = pad_scr[...].T.astype(BF16)
        chunk(knt, vnt, True)

    kt = jnp.concatenate([kpg[i][...].astype(BF16) for i in reversed(range(pp))], axis=1)
    vt = jnp.concatenate([vpg[i][...].astype(BF16) for i in reversed(range(pp))], axis=1)
    chunk(kt, vt, False)

    @pl.when(s == pl.num_programs(1) - 1)
    def _():
        col = lax.broadcasted_iota(jnp.int32, (t_new, a), 1)
        out = jnp.zeros((t_new, a), F32)
        for h in range(nh):
            out = out + jnp.where(col // hd == h, acc_scr[h * t_new:(h + 1) * t_new, :], 0.0)
        o_ref[...] = out.astype(BF16)


def _attn_sample(q, kn, vn, cache_kt, cache_vt, layer, page_table, bias, pp=32):
    b, t, a = q.shape
    n_pages = page_table.shape[1]
    assert n_pages % pp == 0 and t <= PAGE_SIZE
    nq = (a // HEAD_DIM) * t
    new_spec = pl.BlockSpec((None, t, a), lambda bi, si, pt: (bi, 0, 0))

    def page_spec(i):
        return pl.BlockSpec(
            (None, None, a, PAGE_SIZE),
            lambda bi, si, pt: (layer, pt[bi, n_pages - 1 - (si * pp + i)], 0, 0))

    grid_spec = pltpu.PrefetchScalarGridSpec(
        num_scalar_prefetch=1,
        grid=(b, n_pages // pp),
        in_specs=([pl.BlockSpec(memory_space=pltpu.SMEM), new_spec, new_spec, new_spec]
                  + [page_spec(i) for i in range(pp)] * 2),
        out_specs=new_spec,
        scratch_shapes=[
            pltpu.VMEM((nq, a), BF16),
            pltpu.VMEM((nq, 1), F32),
            pltpu.VMEM((nq, 1), F32),
            pltpu.VMEM((nq, a), F32),
            pltpu.VMEM((PAGE_SIZE, a), F32),
        ],
    )
    return pl.pallas_call(
        functools.partial(_attn_sample_body, pp=pp),
        grid_spec=grid_spec,
        out_shape=jax.ShapeDtypeStruct((b, t, a), BF16),
        compiler_params=_params(("parallel", "arbitrary")),
        name="attn_sample",
    )(page_table, bias, q, kn, vn, *([cache_kt] * pp), *([cache_vt] * pp))


def _mix_body(*refs, has_hist, has_prev, pos0):
    if has_prev:
        n_in = 16 + (2 if has_hist else 0)
        refs = refs[:n_in] + refs[n_in + 1:]
    if has_hist:
        (x_ref, u_ref, p_ref, attn_ref, gate_ref, ust_ref, pst_ref, uh_ref, ph_ref,
         cw_ref, cb_ref, lg_ref, lb_ref, wco_ref, wao_ref, pw_ref, ps_ref, wo_ref,
         o_ref, uext, pext, ycv, ush) = refs
    else:
        (x_ref, u_ref, p_ref, attn_ref, gate_ref, ust_ref, pst_ref,
         cw_ref, cb_ref, lg_ref, lb_ref, wco_ref, wao_ref, pw_ref, ps_ref, wo_ref,
         o_ref, uext, pext, ycv, ush) = refs
    i = pl.program_id(1)
    nb, ts, c = u_ref.shape
    m = nb * ts
    d = x_ref.shape[1]

    if has_hist:
        first = i == 0
        uext[:, 0:CONV_HIST, :] = jnp.where(first, ust_ref[...], uh_ref[...])
        pext[:, 0:POOL_HIST, :] = jnp.where(first, pst_ref[...], ph_ref[...])
    else:
        uext[:, 0:CONV_HIST, :] = ust_ref[...]
        pext[:, 0:POOL_HIST, :] = pst_ref[...]
    uext[:, CONV_HIST:CONV_HIST + ts, :] = u_ref[...]
    pext[:, POOL_HIST:POOL_HIST + ts, :] = p_ref[...]

    off = CONV_HIST - (CONV_W - 1)
    for b in range(1, 8):
        ush[b - 1] = uext[:, b:b + ush.shape[2], :]
    rc = min(ts, 64)
    for lg in range(c // LANES):
        ls = slice(lg * LANES, (lg + 1) * LANES)
        for r0 in range(0, ts, rc):
            acc = jnp.zeros((nb, rc, LANES), F32)
            for j in range(CONV_W):
                b = (off + j) % 8
                a = r0 + off + j - b
                win = uext[:, a:a + rc, ls] if b == 0 else ush[b - 1, :, a:a + rc, ls]
                acc = acc + cw_ref[j:j + 1, ls] * win
            ycv[:, r0:r0 + rc, ls] = acc

    yf = ycv[...].reshape(m, c) + cb_ref[...]
    mu = jnp.mean(yf, axis=-1, keepdims=True)
    yc = yf - mu
    var = jnp.mean(yc * yc, axis=-1, keepdims=True)
    yn = yc * lax.rsqrt(var + EPS) * lg_ref[...] + lb_ref[...]
    y_conv = jnp.dot((yn * jax.nn.sigmoid(yn)).astype(BF16), wco_ref[...],
                     preferred_element_type=F32)

    y_attn = jnp.dot(attn_ref[...], wao_ref[...], preferred_element_type=F32)

    pos = pos0 + i * ts + lax.broadcasted_iota(jnp.int32, (1, ts, 1), 1)
    pooled = []
    for gi, win in enumerate(POOL_WINDOWS):
        ls = slice(gi * LANES, (gi + 1) * LANES)
        cur = pext[:, POOL_HIST:POOL_HIST + ts, ls]
        ws = cur
        for k in range(1, win):
            ws = ws + pext[:, POOL_HIST - k:POOL_HIST - k + ts, ls]
        cnt = jnp.minimum(pos + 1, win).astype(F32)
        pg = (ws / cnt - cur).reshape(m, LANES).astype(BF16)
        pooled.append(jnp.dot(pg, pw_ref[gi], preferred_element_type=F32))
    y_pool = jnp.concatenate(pooled, axis=-1) * ps_ref[...]

    g0 = gate_ref[:, 0:d].astype(F32)
    g1 = gate_ref[:, d:2 * d].astype(F32)
    g2 = gate_ref[:, 2 * d:3 * d].astype(F32)
    merged = g0 * y_conv + g1 * y_attn + g2 * y_pool
    o_ref[...] = x_ref[...] + jnp.dot(merged.astype(BF16), wo_ref[...], preferred_element_type=F32)


def _mix(x, u, p, attn, gate, ust, pst, lw, nb, ts, pos0, into=None):
    (conv_w, conv_b, ln_g, ln_b, wco, wao, pool_w, pool_scale, wo) = lw
    b, t, c = u.shape
    d = x.shape[1]
    nt = t // ts
    has_hist = nt > 1
    assert t % ts == 0 and b % nb == 0 and (nb == 1 or not has_hist)
    out_rows, row0, prev = (x.shape[0], 0, None) if into is None else into
    assert row0 % (nb * ts) == 0 and out_rows % (nb * ts) == 0
    blk0 = row0 // (nb * ts)
    tok = lambda bi, ti: (bi * nt + ti, 0)
    tok_out = lambda bi, ti: (bi * nt + ti + blk0, 0)
    seq = lambda bi, ti: (bi, ti, 0)
    in_specs = [
        pl.BlockSpec((nb * ts, d), tok),
        pl.BlockSpec((nb, ts, c), seq),
        pl.BlockSpec((nb, ts, c), seq),
        pl.BlockSpec((nb * ts, c), tok),
        pl.BlockSpec((nb * ts, 3 * d), tok),
        pl.BlockSpec((nb, CONV_HIST, c), lambda bi, ti: (bi, 0, 0)),
        pl.BlockSpec((nb, POOL_HIST, c), lambda bi, ti: (bi, 0, 0)),
    ]
    args = [x, u, p, attn, gate, ust, pst]
    if has_hist:
        in_specs += [
            pl.BlockSpec((nb, CONV_HIST, c),
                         lambda bi, ti: (bi, jnp.maximum(ti * (ts // CONV_HIST) - 1, 0), 0)),
            pl.BlockSpec((nb, POOL_HIST, c),
                         lambda bi, ti: (bi, jnp.maximum(ti * (ts // POOL_HIST) - 1, 0), 0)),
        ]
        args += [u, p]
    consts = [conv_w, conv_b.reshape(1, c), ln_g.reshape(1, c), ln_b.reshape(1, c),
              wco, wao, pool_w, pool_scale.reshape(1, d), wo]
    in_specs += [_resident(a.shape) for a in consts]
    args += consts
    aliases = {}
    if prev is not None:
        aliases = {len(args): 0}
        in_specs.append(pl.BlockSpec(memory_space=pl.ANY))
        args.append(prev)
    return pl.pallas_call(
        functools.partial(_mix_body, has_hist=has_hist, has_prev=prev is not None, pos0=pos0),
        grid=(b // nb, nt),
        in_specs=in_specs,
        out_specs=pl.BlockSpec((nb * ts, d), tok_out),
        out_shape=jax.ShapeDtypeStruct((out_rows, d), F32),
        input_output_aliases=aliases,
        scratch_shapes=[pltpu.VMEM((nb, CONV_HIST + ts, c), F32),
                        pltpu.VMEM((nb, POOL_HIST + ts, c), F32),
                        pltpu.VMEM((nb, ts, c), F32),
                        pltpu.VMEM((7, nb, CONV_HIST + ts - 8, c), F32)],
        compiler_params=_params(("parallel", "parallel")),
        name="mix",
    )(*args)


def _swiglu_into(hn, w1_ref, w3_ref, a_scr, fc):
    f = a_scr.shape[1]
    for c0 in range(0, f, fc):
        n = min(fc, f - c0)
        h1 = jnp.dot(hn, w1_ref[:, c0:c0 + n], preferred_element_type=F32)
        h3 = jnp.dot(hn, w3_ref[:, c0:c0 + n], preferred_element_type=F32)
        a_scr[:, c0:c0 + n] = (h1 * jax.nn.sigmoid(h1) * h3).astype(BF16)


def _ffn_body(x_ref, g_ref, w1_ref, w3_ref, w2_ref, o_ref, a_scr, *, fc):
    x = x_ref[...]
    hn = _rms(x, g_ref[...]).astype(BF16)
    _swiglu_into(hn, w1_ref, w3_ref, a_scr, fc)
    o_ref[...] = x + jnp.dot(a_scr[...], w2_ref[...], preferred_element_type=F32)


def _ffn(x, g, w1, w3, w2, tm):
    n, d = x.shape
    assert n % tm == 0
    f = w1.shape[1]
    row = lambda i: (i, 0)
    return pl.pallas_call(
        functools.partial(_ffn_body, fc=512),
        grid=(n // tm,),
        in_specs=[pl.BlockSpec((tm, d), row), _resident((1, d)),
                  _resident(w1.shape), _resident(w3.shape), _resident(w2.shape)],
        out_specs=pl.BlockSpec((tm, d), row),
        out_shape=jax.ShapeDtypeStruct((n, d), F32),
        scratch_shapes=[pltpu.VMEM((tm, f), BF16)],
        compiler_params=_params(("parallel",)),
        name="ffn",
    )(x, g.reshape(1, d), w1, w3, w2)


MOE_CHUNK = 1280
MOE_TILE = 256
MOE_SUB = 256
M_E1, M_E2, M_POS1, M_POS2, M_P1, M_P2 = range(6)
M_FIELDS = 8


def _route_top2(hn_f32, wr_ref, n_exp):
    xh, xl = _split_bf16(hn_f32)
    wh, wl = _split_bf16(wr_ref[...])
    logits = (jnp.dot(xh, wh, preferred_element_type=F32)
              + jnp.dot(xh, wl, preferred_element_type=F32)
              + jnp.dot(xl, wh, preferred_element_type=F32))
    lane = lax.broadcasted_iota(jnp.int32, logits.shape, 1)
    neg = jnp.float32(-jnp.inf)
    l1 = jnp.where(lane < n_exp, logits, neg)
    m1 = jnp.max(l1, axis=-1, keepdims=True)
    i1 = jnp.min(jnp.where(l1 == m1, lane, LANES), axis=-1, keepdims=True)
    l2 = jnp.where(lane == i1, neg, l1)
    m2 = jnp.max(l2, axis=-1, keepdims=True)
    i2 = jnp.min(jnp.where(l2 == m2, lane, LANES), axis=-1, keepdims=True)
    e2 = jnp.exp(m2 - m1)
    return i1, i2, 1.0 / (1.0 + e2), e2 / (1.0 + e2)


def _route_body(x_ref, g_ref, wr_ref, hn_ref, meta_ref, metat_ref, cnt_ref, run_scr, *, n_exp, sub):
    @pl.when(pl.program_id(0) == 0)
    def _():
        run_scr[...] = jnp.zeros_like(run_scr)

    hn = _rms(x_ref[...], g_ref[...])
    hn_ref[...] = hn.astype(BF16)
    i1, i2, p1, p2 = _route_top2(hn, wr_ref, n_exp)
    m = hn.shape[0]
    lane = lax.broadcasted_iota(jnp.int32, (m, LANES), 1)
    sel1 = lane == i1
    sel2 = lane == i2
    assign = (sel1 | sel2).astype(BF16)
    row = lax.broadcasted_iota(jnp.int32, (sub, sub), 0)
    col = lax.broadcasted_iota(jnp.int32, (sub, sub), 1)
    before = (col < row).astype(BF16)
    run = run_scr[0:1, :]
    ranks = []
    for k, r0 in enumerate(range(0, m, sub)):
        a = assign[r0:r0 + sub, :]
        ranks.append(jnp.dot(before, a, preferred_element_type=F32) + run)
        run = run + jnp.sum(a.astype(F32), axis=0, keepdims=True)
        cnt_ref[k] = jnp.broadcast_to(run, cnt_ref.shape[1:])
    rank = jnp.concatenate(ranks, axis=0)
    run_scr[...] = jnp.broadcast_to(run, run_scr.shape)
    pos1 = jnp.sum(jnp.where(sel1, rank, 0.0), axis=-1, keepdims=True)
    pos2 = jnp.sum(jnp.where(sel2, rank, 0.0), axis=-1, keepdims=True)
    fields = {M_E1: i1.astype(F32), M_E2: i2.astype(F32), M_POS1: pos1, M_POS2: pos2,
              M_P1: p1, M_P2: p2}
    meta = jnp.zeros((m, LANES), F32)
    for j, val in fields.items():
        meta = jnp.where(lane == j, val, meta)
    meta_ref[...] = meta
    metat_ref[...] = meta.T[0:M_FIELDS, :]


def _route(x, g, wr, chunk, sub):
    n, d = x.shape
    n_exp = wr.shape[1]
    wr_pad = jnp.zeros((d, LANES), F32).at[:, :n_exp].set(wr)
    assert n % chunk == 0 and chunk % sub == 0
    nc = n // chunk
    spc = chunk // sub
    row = lambda c: (c, 0)
    return pl.pallas_call(
        functools.partial(_route_body, n_exp=n_exp, sub=sub),
        grid=(nc,),
        in_specs=[pl.BlockSpec((chunk, d), row), _resident((1, d)), _resident((d, LANES))],
        out_specs=[pl.BlockSpec((chunk, d), row), pl.BlockSpec((chunk, LANES), row),
                   pl.BlockSpec((M_FIELDS, chunk), lambda c: (0, c)),
                   pl.BlockSpec((spc, 8, LANES), lambda c: (c, 0, 0))],
        out_shape=[jax.ShapeDtypeStruct((n, d), BF16),
                   jax.ShapeDtypeStruct((n, LANES), F32),
                   jax.ShapeDtypeStruct((M_FIELDS, n), F32),
                   jax.ShapeDtypeStruct((n // sub, 8, LANES), F32)],
        scratch_shapes=[pltpu.VMEM((8, LANES), F32)],
        compiler_params=_params(("arbitrary",)),
        name="moe_route",
    )(x, g.reshape(1, d), wr_pad)


GATHER_RING = 3


def _gather_body(ptile, pchunk, pfirst, pvalid, texp, trank0, hn_hbm, metat_ref, o_ref,
                 acc_scr, ring, sems):
    i = pl.program_id(0)
    chunk = ring.shape[1]

    def chunk_copy(j):
        rows = pl.ds(pl.multiple_of(pchunk[j] * chunk, chunk), chunk)
        return pltpu.make_async_copy(hn_hbm.at[rows], ring.at[j % GATHER_RING],
                                     sems.at[j % GATHER_RING])

    @pl.when(i == 0)
    def _():
        for j in range(GATHER_RING - 1):
            chunk_copy(j).start()

    @pl.when(i + GATHER_RING - 1 < pl.num_programs(0))
    def _():
        chunk_copy(i + GATHER_RING - 1).start()

    chunk_copy(i).wait()
    hn_ref = ring.at[i % GATHER_RING]

    @pl.when(pvalid[i] == 1)
    def _():
        r = ptile[i]
        e = texp[r]
        mt = metat_ref[...]
        e1 = mt[M_E1:M_E1 + 1, :].astype(jnp.int32)
        e2 = mt[M_E2:M_E2 + 1, :].astype(jnp.int32)
        pos = jnp.where(e1 == e, mt[M_POS1:M_POS1 + 1, :],
                        jnp.where(e2 == e, mt[M_POS2:M_POS2 + 1, :], -1.0))
        slot = pos.astype(jnp.int32) - trank0[r]
        rows = lax.broadcasted_iota(jnp.int32, (o_ref.shape[0], slot.shape[1]), 0)
        onehot = (rows == slot).astype(BF16)
        res = jnp.dot(onehot, hn_ref[...], preferred_element_type=F32)

        @pl.when(pfirst[i] == 1)
        def _():
            acc_scr[...] = res

        @pl.when(pfirst[i] == 0)
        def _():
            acc_scr[...] += res

        o_ref[...] = acc_scr[...].astype(BF16)


def _expert_body(texp, nvalid, xs_ref, w1_ref, w3_ref, w2_ref, y_ref, a_scr, *, fc):
    @pl.when(pl.program_id(0) < nvalid[0])
    def _():
        _swiglu_into(xs_ref[...], w1_ref, w3_ref, a_scr, fc)
        y_ref[...] = jnp.dot(a_scr[...], w2_ref[...], preferred_element_type=F32).astype(BF16)


def _combine_body(ptile, pchunk, pfirst, plast, pvalid, pbits, texp, trank0,
                  x_ref, meta_ref, y_ref, gf_ref, o_ref, acc_scr, *, final_norm, sub):
    i = pl.program_id(0)

    @pl.when(pvalid[i] == 1)
    def _():
        r = ptile[i]
        e = texp[r]
        m = meta_ref[...]
        is1 = m[:, M_E1:M_E1 + 1].astype(jnp.int32) == e
        is2 = m[:, M_E2:M_E2 + 1].astype(jnp.int32) == e
        pos = jnp.where(is1, m[:, M_POS1:M_POS1 + 1],
                        jnp.where(is2, m[:, M_POS2:M_POS2 + 1], -1.0))
        gate = jnp.where(is1, m[:, M_P1:M_P1 + 1], jnp.where(is2, m[:, M_P2:M_P2 + 1], 0.0))
        slot = pos.astype(jnp.int32) - trank0[r]

        @pl.when(pfirst[i] == 1)
        def _():
            acc_scr[...] = jnp.zeros_like(acc_scr)

        cols = lax.broadcasted_iota(jnp.int32, (sub, y_ref.shape[0]), 1)
        for k, r0 in enumerate(range(0, m.shape[0], sub)):
            @pl.when(((pbits[i] >> k) & 1) == 1)
            def _(r0=r0):
                onehot = (cols == slot[r0:r0 + sub, :]).astype(BF16)
                acc_scr[r0:r0 + sub, :] += gate[r0:r0 + sub, :] * jnp.dot(
                    onehot, y_ref[...], preferred_element_type=F32)

        @pl.when(plast[i] == 1)
        def _():
            out = x_ref[...] + acc_scr[...]
            if final_norm:
                out = _rms(out, gf_ref[...])
            o_ref[...] = out


def _pair_list(hit, npairs):
    nb = hit.shape[1]
    idx = jnp.nonzero(hit.reshape(-1), size=npairs, fill_value=-1)[0].astype(jnp.int32)
    valid = idx >= 0
    count = jnp.sum(valid.astype(jnp.int32))
    idx = jnp.where(valid, idx, idx[jnp.maximum(count - 1, 0)])
    a = idx // nb
    b = idx % nb
    off = jnp.full((1,), -1, jnp.int32)
    first = valid & (a != jnp.concatenate([off, a[:-1]]))
    nxt_valid = jnp.concatenate([valid[1:], jnp.zeros((1,), bool)])
    last = valid & ((a != jnp.concatenate([a[1:], off])) | ~nxt_valid)
    i32 = lambda v: v.astype(jnp.int32)
    return idx, a, b, i32(first), i32(last), i32(valid)


def _moe(x, g, wr, w1, w3, w2, gf, final_norm, chunk=MOE_CHUNK, tg=MOE_TILE, sub=MOE_SUB):
    n, d = x.shape
    n_exp, _, f = w1.shape
    nc = n // chunk
    spc = chunk // sub
    nt = (TOP_K * n + n_exp * (tg - 1)) // tg
    npairs = nt + n_exp * nc
    assert npairs >= GATHER_RING

    hn, meta, metat, cnt = _route(x, g, wr, chunk, sub)

    after = cnt[:, 0, :n_exp].astype(jnp.int32)
    cb = jnp.concatenate([jnp.zeros((1, n_exp), jnp.int32), after], axis=0)
    tiles_e = (cb[-1] + tg - 1) // tg
    tend = jnp.cumsum(tiles_e)
    nvalid = tend[-1]
    r = jnp.arange(nt, dtype=jnp.int32)
    rc = jnp.minimum(r, nvalid - 1)
    texp = jnp.sum((rc[:, None] >= tend[None, :]).astype(jnp.int32), axis=1)
    trank0 = (rc - (tend - tiles_e)[texp]) * tg
    lo = cb[:-1][:, texp].T
    hi = cb[1:][:, texp].T
    sub_hit = ((r < nvalid)[:, None] & (hi > lo)
               & (lo < trank0[:, None] + tg) & (hi > trank0[:, None])).reshape(nt, nc, spc)
    hit = jnp.any(sub_hit, axis=2)
    bits = jnp.sum(sub_hit.astype(jnp.int32) << jnp.arange(spc, dtype=jnp.int32), axis=2)
    _, g_tile, g_chunk, g_first, _, g_valid = _pair_list(hit, npairs)
    c_idx, c_chunk, c_tile, c_first, c_last, c_valid = _pair_list(hit.T, npairs)
    c_bits = bits.T.reshape(-1)[c_idx] * c_valid
    nvalid1 = nvalid.reshape(1).astype(jnp.int32)

    xs = pl.pallas_call(
        _gather_body,
        grid_spec=pltpu.PrefetchScalarGridSpec(
            num_scalar_prefetch=6, grid=(npairs,),
            in_specs=[pl.BlockSpec(memory_space=pl.ANY),
                      pl.BlockSpec((M_FIELDS, chunk), lambda i, pt, pc, *_: (0, pc[i]))],
            out_specs=pl.BlockSpec((tg, d), lambda i, pt, *_: (pt[i], 0)),
            scratch_shapes=[pltpu.VMEM((tg, d), F32), pltpu.VMEM((GATHER_RING, chunk, d), BF16),
                            pltpu.SemaphoreType.DMA((GATHER_RING,))]),
        out_shape=jax.ShapeDtypeStruct((nt * tg, d), BF16),
        compiler_params=_params(("arbitrary",)),
        name="moe_gather",
    )(g_tile, g_chunk, g_first, g_valid, texp, trank0, hn, metat)

    tile = lambda i, te, nv: (jnp.minimum(i, nv[0] - 1), 0)
    wexp = lambda i, te, nv: (te[i], 0, 0)
    y = pl.pallas_call(
        functools.partial(_expert_body, fc=512),
        grid_spec=pltpu.PrefetchScalarGridSpec(
            num_scalar_prefetch=2, grid=(nt,),
            in_specs=[pl.BlockSpec((tg, d), tile), pl.BlockSpec((None, d, f), wexp),
                      pl.BlockSpec((None, d, f), wexp), pl.BlockSpec((None, f, d), wexp)],
            out_specs=pl.BlockSpec((tg, d), tile),
            scratch_shapes=[pltpu.VMEM((tg, f), BF16)]),
        out_shape=jax.ShapeDtypeStruct((nt * tg, d), BF16),
        compiler_params=_params(("arbitrary",)),
        name="moe_expert",
    )(texp, nvalid1, xs, w1, w3, w2)

    return pl.pallas_call(
        functools.partial(_combine_body, final_norm=final_norm, sub=sub),
        grid_spec=pltpu.PrefetchScalarGridSpec(
            num_scalar_prefetch=8, grid=(npairs,),
            in_specs=[pl.BlockSpec((chunk, d), lambda i, pt, pc, *_: (pc[i], 0)),
                      pl.BlockSpec((chunk, LANES), lambda i, pt, pc, *_: (pc[i], 0)),
                      pl.BlockSpec((tg, d), lambda i, pt, *_: (pt[i], 0)),
                      pl.BlockSpec((1, d), lambda i, *_: (0, 0))],
            out_specs=pl.BlockSpec((chunk, d), lambda i, pt, pc, *_: (pc[i], 0)),
            scratch_shapes=[pltpu.VMEM((chunk, d), F32)]),
        out_shape=jax.ShapeDtypeStruct((n, d), F32),
        compiler_params=_params(("arbitrary",)),
        name="moe_combine",
    )(c_tile, c_chunk, c_first, c_last, c_valid, c_bits, texp, trank0,
      x, meta, y, gf.reshape(1, d))


def _norm_body(x_ref, g_ref, o_ref):
    o_ref[...] = _rms(x_ref[...], g_ref[...])


def _final_norm(x, g, tm):
    n, d = x.shape
    row = lambda i: (i, 0)
    return pl.pallas_call(
        _norm_body,
        grid=(n // tm,),
        in_specs=[pl.BlockSpec((tm, d), row), _resident((1, d))],
        out_specs=pl.BlockSpec((tm, d), row),
        out_shape=jax.ShapeDtypeStruct((n, d), F32),
        compiler_params=_params(("parallel",)),
        name="final_norm",
    )(x, g.reshape(1, d))


def _pad_front(state, rows):
    b, r, c = state.shape
    return jnp.concatenate([jnp.zeros((b, rows - r, c), state.dtype), state], axis=1)


def kernel(x_prompt, x_sample, cache_k, cache_v, state_conv, state_pool, page_table, norm_mix, w_in, conv_w, conv_b, conv_ln_g, conv_ln_b, w_conv_out, w_attn_out, sb_bias, pool_w, pool_scale, w_o, norm_ffn, ffn_w1, ffn_w3, ffn_w2, moe_router, moe_w1, moe_w3, moe_w2, norm_final):
    bp, tp, d = x_prompt.shape
    bs, tsm, _ = x_sample.shape
    depth = w_in.shape[0]
    a = N_HEADS * HEAD_DIM
    n_pages = page_table.shape[1]
    past = n_pages * PAGE_SIZE
    cshape = (cache_k.shape[0], cache_k.shape[1], a, PAGE_SIZE)
    ck = jnp.transpose(cache_k, (0, 1, 3, 4, 2)).reshape(cshape)
    cv = jnp.transpose(cache_v, (0, 1, 3, 4, 2)).reshape(cshape)

    xp = x_prompt.reshape(bp * tp, d)
    xs = x_sample.reshape(bs * tsm, d)
    outs = {k: [] for k in ("cp", "pp", "ks", "vs", "cs", "ps")}
    kv_prompt = None
    for l in range(depth):
        w_in_l = w_in[l].astype(BF16)
        lw = (conv_w[l], conv_b[l], conv_ln_g[l], conv_ln_b[l], w_conv_out[l].astype(BF16),
              w_attn_out[l].astype(BF16), pool_w[l].astype(BF16), pool_scale[l], w_o[l].astype(BF16))

        u, q, kt_p, vt_p, p, gate = _in_proj(xp, norm_mix[l], w_in_l, tm=512,
                                             kv_seq=(l, depth, bp, tp, kv_prompt))
        kv_prompt = (kt_p, vt_p)
        c = u.shape[1]
        attn = _attn_prompt(q.reshape(bp, tp, a), kt_p, vt_p, l, sb_bias[l])
        u3 = u.reshape(bp, tp, c)
        p3 = p.reshape(bp, tp, c)
        routed = l % 2 == 1
        n_all = bp * tp + bs * tsm
        xp = _mix(xp, u3, p3, attn.reshape(bp * tp, a), gate,
                  jnp.zeros((bp, CONV_HIST, c), F32), jnp.zeros((bp, POOL_HIST, c), F32),
                  lw, nb=1, ts=256, pos0=0, into=(n_all, 0, None) if routed else None)
        outs["cp"].append(u3[:, tp - (CONV_W - 1):])
        outs["pp"].append(p3[:, tp - (POOL_MAX - 1):])

        u, q, k, v, p, gate = _in_proj(xs, norm_mix[l], w_in_l, tm=bs * tsm)
        attn = _attn_sample(q.reshape(bs, tsm, a), k.reshape(bs, tsm, a), v.reshape(bs, tsm, a),
                            ck, cv, l, page_table, sb_bias[l])
        u3 = u.reshape(bs, tsm, c)
        p3 = p.reshape(bs, tsm, c)
        xs = _mix(xs, u3, p3, attn.reshape(bs * tsm, a), gate,
                  _pad_front(state_conv[l], CONV_HIST), _pad_front(state_pool[l], POOL_HIST),
                  lw, nb=bs, ts=tsm, pos0=past, into=(n_all, bp * tp, xp) if routed else None)
        outs["ks"].append(k.reshape(bs, tsm, N_HEADS, HEAD_DIM))
        outs["vs"].append(v.reshape(bs, tsm, N_HEADS, HEAD_DIM))
        outs["cs"].append(jnp.concatenate([state_conv[l], u3], axis=1)[:, -(CONV_W - 1):])
        outs["ps"].append(jnp.concatenate([state_pool[l], p3], axis=1)[:, -(POOL_MAX - 1):])

        i = l // 2
        last = l == depth - 1
        if l % 2 == 0:
            w1, w3, w2 = ffn_w1[i].astype(BF16), ffn_w3[i].astype(BF16), ffn_w2[i].astype(BF16)
            xp = _ffn(xp, norm_ffn[l], w1, w3, w2, tm=512)
            xs = _ffn(xs, norm_ffn[l], w1, w3, w2, tm=bs * tsm)
            if last:
                xp = _final_norm(xp, norm_final, tm=512)
                xs = _final_norm(xs, norm_final, tm=bs * tsm)
        else:
            w1, w3, w2 = moe_w1[i].astype(BF16), moe_w3[i].astype(BF16), moe_w2[i].astype(BF16)
            x_all = _moe(xs, norm_ffn[l], moe_router[i], w1, w3, w2, norm_final, last)
            xp, xs = x_all[:bp * tp], x_all[bp * tp:]

    k_prompt, v_prompt = (
        jnp.transpose(t.reshape(depth, bp, N_HEADS, HEAD_DIM, tp), (0, 1, 4, 2, 3))
        for t in kv_prompt)
    return (xp.reshape(bp, tp, d), xs.reshape(bs, tsm, d),
            k_prompt, v_prompt, jnp.stack(outs["cp"]), jnp.stack(outs["pp"]),
            jnp.stack(outs["ks"]), jnp.stack(outs["vs"]), jnp.stack(outs["cs"]), jnp.stack(outs["ps"]))
```
